```python
import math
import jax, jax.numpy as jnp
from jax import lax
import numpy as np

D_MODEL = 1024
BATCH = 8
SEQ = 4096
DEPTH = 2
DEC_BATCH = 32
DEC_SEQ = 4
PAST_LEN = 16384
PAGE_SIZE = 128

MIX_WIDTH = D_MODEL
HG_WIDTH = MIX_WIDTH // 4
HG_HEADS = 4
HG_DIM = HG_WIDTH // HG_HEADS
HG_CHUNK = 64
GM_WIDTH = MIX_WIDTH // 4
GM_GROUPS = 4
GM_DIM = GM_WIDTH // GM_GROUPS
GM_CHUNK = 128
ATT_WIDTH = MIX_WIDTH // 2
ATT_HEADS = 8
ATT_DIM = ATT_WIDTH // ATT_HEADS
IDX_HEADS = 8
IDX_DIM = 64
TOPK_MAX = 256
Q_BLOCK = 128
MEM_LEN = 256
X_HEADS = 4
X_WIDTH = D_MODEL
X_DIM = X_WIDTH // X_HEADS
RMS_EPS = 1e-6
PROJ_SPLITS = (HG_WIDTH,) * 4 + (GM_WIDTH,) * 3 + (ATT_WIDTH,) * 4 + (IDX_HEADS * IDX_DIM, IDX_DIM, IDX_HEADS)
N_IN = sum(PROJ_SPLITS)

kernel_name = 'hymba_hgrn2_gmlp_dsa_decoder_step'


def rms_norm(x, g):
    xf = x.astype(jnp.float32)
    y = xf * lax.rsqrt(jnp.mean(xf * xf, axis=-1, keepdims=True) + RMS_EPS)
    return (y * g.astype(jnp.float32)).astype(x.dtype)


def project(h, w_in, lb):
    bn, length, _ = h.shape
    z = h @ w_in
    cuts = []
    acc = 0
    for s in PROJ_SPLITS[:-1]:
        acc += s
        cuts.append(acc)
    (a_q, a_f, a_i, a_g, b_u, b_v, b_g, c_q, c_k, c_v, c_g, c_qi, c_ki, c_wi) = jnp.split(z, cuts, axis=-1)
    f = lb + (1.0 - lb) * jax.nn.sigmoid(a_f.astype(jnp.float32))
    heads = lambda t, nh: t.reshape(bn, length, nh, -1)
    return {
        'hg_q': heads(a_q, HG_HEADS), 'hg_k': heads(1.0 - f, HG_HEADS), 'hg_v': heads(a_i, HG_HEADS),
        'hg_logf': heads(jnp.log(f), HG_HEADS), 'hg_gate': a_g,
        'gm_u': jax.nn.gelu(b_u), 'gm_v': jax.nn.gelu(b_v), 'gm_gate': b_g,
        'q': heads(c_q, ATT_HEADS), 'k': heads(c_k, ATT_HEADS), 'v': heads(c_v, ATT_HEADS), 'att_gate': c_g,
        'qi': heads(c_qi, IDX_HEADS), 'ki': c_ki, 'wi': c_wi,
    }


def hgrn2_recurrence(q, k, v, logf, s0):
    bn, length, nh, _ = q.shape
    dv = v.shape[-1]
    c = math.gcd(length, HG_CHUNK)
    nc = length // c
    def chunks(a):
        return a.astype(jnp.float32).reshape(bn, nc, c, nh, a.shape[-1]).transpose(1, 0, 3, 2, 4)
    causal = jnp.tril(jnp.ones((c, c), dtype=bool))
    def step(s, inp):
        qc, kc, vc, gc = inp
        b = jnp.cumsum(gc, axis=2)
        o_inter = jnp.einsum('bhtk,bhkv->bhtv', qc * jnp.exp(b), s)
        diff = b[:, :, :, None, :] - b[:, :, None, :, :]
        decay = jnp.exp(jnp.where(causal[:, :, None], diff, -jnp.inf))
        att = jnp.einsum('bhtk,bhtsk,bhsk->bhts', qc, decay, kc)
        o = o_inter + jnp.einsum('bhts,bhsv->bhtv', att, vc)
        b_last = b[:, :, -1:, :]
        s = jnp.exp(b_last[:, :, 0, :])[..., None] * s + jnp.einsum('bhsk,bhsv->bhkv', kc * jnp.exp(b_last - b), vc)
        return s, o
    s_fin, o = lax.scan(step, s0.astype(jnp.float32), (chunks(q), chunks(k), chunks(v), chunks(logf)))
    o = o.transpose(1, 0, 3, 2, 4).reshape(bn, length, nh, dv)
    return o.astype(v.dtype), s_fin


def gmlp_spatial(vn, w_s, b_s):
    bn, length = vn.shape[:2]
    padded = -(-length // GM_CHUNK) * GM_CHUNK
    vp = jnp.pad(vn, ((0, 0), (0, padded - length), (0, 0), (0, 0)))
    vc = vp.reshape(bn, padded // GM_CHUNK, GM_CHUNK, GM_GROUPS, GM_DIM)
    w_causal = w_s * jnp.tril(jnp.ones((GM_CHUNK, GM_CHUNK), w_s.dtype))
    mixed = jnp.einsum('gts,bcsgd->bctgd', w_causal, vc) + b_s.T[None, None, :, :, None]
    return mixed.reshape(bn, padded, GM_GROUPS, GM_DIM)[:, :length]


def gmlp_branch(u, v, g_v, w_s, b_s):
    bn, length = u.shape[:2]
    vn = rms_norm(v, g_v)
    mixed = gmlp_spatial(vn.reshape(bn, length, GM_GROUPS, GM_DIM), w_s, b_s)
    return u * mixed.reshape(bn, length, GM_WIDTH), vn


def gather_rows(a, idx):
    return jax.vmap(lambda ar, ir: ar[ir])(a, idx)


def dsa_select(qi, wi, ki, q_pos, k_pos, k_top):
    s = jnp.einsum('bthd,bsd->bths', qi.astype(jnp.float32), ki.astype(jnp.float32)) * (IDX_DIM ** -0.5)
    score = jnp.einsum('bths,bth->bts', jax.nn.relu(s), wi.astype(jnp.float32)) * (IDX_HEADS ** -0.5)
    allowed = k_pos[None, :] <= q_pos[:, None]
    score = jnp.where(allowed[None], score, -jnp.inf)
    _, idx = lax.top_k(score, k_top)
    valid = idx <= q_pos[None, :, None]
    return idx, valid


def sparse_attend(q, k_sel, v_sel, valid):
    logits = jnp.einsum('bthd,btkhd->bthk', q.astype(jnp.float32), k_sel.astype(jnp.float32)) * (ATT_DIM ** -0.5)
    logits = jnp.where(valid[:, :, None, :], logits, -jnp.inf)
    p = jax.nn.softmax(logits, axis=-1)
    return jnp.einsum('bthk,btkhd->bthd', p, v_sel.astype(jnp.float32)).astype(q.dtype)


def dsa_prompt(p):
    q, k, v, qi, ki, wi = p['q'], p['k'], p['v'], p['qi'], p['ki'], p['wi']
    bn, length = q.shape[:2]
    k_top = min(TOPK_MAX, length // 4)
    nb = length // Q_BLOCK
    def blocks(a):
        return a.reshape(bn, nb, Q_BLOCK, *a.shape[2:]).swapaxes(0, 1)
    k_pos = jnp.arange(length)
    def one(inp):
        q_b, qi_b, wi_b, t_pos = inp
        idx, valid = dsa_select(qi_b, wi_b, ki, t_pos, k_pos, k_top)
        return sparse_attend(q_b, gather_rows(k, idx), gather_rows(v, idx), valid)
    out = lax.map(one, (blocks(q), blocks(qi), blocks(wi), k_pos.reshape(nb, Q_BLOCK)))
    return out.swapaxes(0, 1).reshape(bn, length, ATT_HEADS, ATT_DIM)


def dsa_sample(p, cache_k, cache_v, cache_kidx, layer, page_table):
    bn, n = p['q'].shape[:2]
    past = page_table.shape[1] * PAGE_SIZE
    ki_past = cache_kidx[layer, page_table].reshape(bn, past, IDX_DIM)
    ki_all = jnp.concatenate([ki_past.astype(p['ki'].dtype), p['ki']], axis=1)
    n_keys = past + n
    k_top = min(TOPK_MAX, n_keys // 4)
    idx, valid = dsa_select(p['qi'], p['wi'], ki_all, past + jnp.arange(n), jnp.arange(n_keys), k_top)
    is_past = idx < past
    pidx = jnp.minimum(idx, past - 1)
    phys = jnp.take_along_axis(page_table, (pidx // PAGE_SIZE).reshape(bn, -1), axis=1).reshape(idx.shape)
    off = pidx % PAGE_SIZE
    nidx = jnp.clip(idx - past, 0, n - 1)
    def pick(pool, new):
        return jnp.where(is_past[..., None, None], pool[layer, phys, off].astype(new.dtype), gather_rows(new, nidx))
    return sparse_attend(p['q'], pick(cache_k, p['k']), pick(cache_v, p['v']), valid)


def merge(p, o_hg, g_hg, gm_out, att_out, w_out):
    bn, length = gm_out.shape[:2]
    y_hg = rms_norm(o_hg, g_hg.reshape(HG_HEADS, HG_DIM)).reshape(bn, length, HG_WIDTH)
    cat = jnp.concatenate([
        y_hg * jax.nn.silu(p['hg_gate']),
        gm_out * jax.nn.silu(p['gm_gate']),
        att_out.reshape(bn, length, ATT_WIDTH) * jax.nn.silu(p['att_gate']),
    ], axis=-1)
    return cat @ w_out


def mem_kv(mem, g_mem, w_k, w_v):
    bn, m, _ = mem.shape
    mn = rms_norm(mem, g_mem)
    return (mn @ w_k).reshape(bn, m, X_HEADS, X_DIM), (mn @ w_v).reshape(bn, m, X_HEADS, X_DIM)


def cross_attend(h, mk, mv, w_q, w_o):
    bn, length, _ = h.shape
    q = (h @ w_q).reshape(bn, length, X_HEADS, X_DIM)
    logits = jnp.einsum('blhd,bmhd->bhlm', q.astype(jnp.float32), mk.astype(jnp.float32)) * (X_DIM ** -0.5)
    pr = jax.nn.softmax(logits, axis=-1)
    o = jnp.einsum('bhlm,bmhd->blhd', pr, mv.astype(jnp.float32)).astype(h.dtype)
    return o.reshape(bn, length, X_WIDTH) @ w_o


def setup_inputs(seed: int = 0) -> dict:
    key = jax.random.key(seed)
    ks = jax.random.split(key, 32)
    n_pages = PAST_LEN // PAGE_SIZE
    n_pool = (DEC_BATCH * n_pages * 5) // 4
    nrm = lambda k, shape, scale: jax.random.normal(k, shape, jnp.float32) * scale
    gain = lambda k, shape: 1.0 + 0.02 * jax.random.normal(k, shape, jnp.float32)
    page_table = jax.random.permutation(ks[8], n_pool)[: DEC_BATCH * n_pages].reshape(DEC_BATCH, n_pages).astype(jnp.int32)
    return {
        'x_prompt': nrm(ks[0], (BATCH, SEQ, D_MODEL), 1.0),
        'x_sample': nrm(ks[1], (DEC_BATCH, DEC_SEQ, D_MODEL), 1.0),
        'cache_k': nrm(ks[2], (DEPTH, n_pool, PAGE_SIZE, ATT_HEADS, ATT_DIM), 1.0),
        'cache_v': nrm(ks[3], (DEPTH, n_pool, PAGE_SIZE, ATT_HEADS, ATT_DIM), 1.0),
        'cache_kidx': nrm(ks[4], (DEPTH, n_pool, PAGE_SIZE, IDX_DIM), 1.0),
        'cache_mem_k': nrm(ks[5], (DEPTH, DEC_BATCH, MEM_LEN, X_HEADS, X_DIM), 1.0),
        'cache_mem_v': nrm(ks[6], (DEPTH, DEC_BATCH, MEM_LEN, X_HEADS, X_DIM), 1.0),
        'state_hgrn': nrm(ks[7], (DEPTH, DEC_BATCH, HG_HEADS, HG_DIM, HG_DIM), 0.5),
        'page_table': page_table,
        'mem_prompt': nrm(ks[9], (BATCH, MEM_LEN, D_MODEL), 1.0),
        'g_mix': gain(ks[10], (DEPTH, D_MODEL)),
        'w_in': nrm(ks[11], (DEPTH, D_MODEL, N_IN), D_MODEL ** -0.5),
        'hg_lb': nrm(ks[12], (DEPTH, HG_WIDTH), 0.5),
        'g_hg': gain(ks[13], (DEPTH, HG_WIDTH)),
        'g_gm': gain(ks[14], (DEPTH, GM_WIDTH)),
        'w_spatial': nrm(ks[15], (DEPTH, GM_GROUPS, GM_CHUNK, GM_CHUNK), GM_CHUNK ** -0.5),
        'b_spatial': gain(ks[16], (DEPTH, GM_GROUPS, GM_CHUNK)),
        'w_out': nrm(ks[17], (DEPTH, MIX_WIDTH, D_MODEL), MIX_WIDTH ** -0.5),
        'g_xattn': gain(ks[18], (DEPTH, D_MODEL)),
        'g_mem': gain(ks[19], (DEPTH, D_MODEL)),
        'w_xq': nrm(ks[20], (DEPTH, D_MODEL, X_WIDTH), D_MODEL ** -0.5),
        'w_xk': nrm(ks[21], (DEPTH, D_MODEL, X_WIDTH), D_MODEL ** -0.5),
        'w_xv': nrm(ks[22], (DEPTH, D_MODEL, X_WIDTH), D_MODEL ** -0.5),
        'w_xo': nrm(ks[23], (DEPTH, X_WIDTH, D_MODEL), X_WIDTH ** -0.5),
        'g_final': gain(ks[24], (D_MODEL,)),
    }


def reference(x_prompt, x_sample, cache_k, cache_v, cache_kidx, cache_mem_k, cache_mem_v, state_hgrn, page_table, mem_prompt,
              g_mix, w_in, hg_lb, g_hg, g_gm, w_spatial, b_spatial, w_out, g_xattn, g_mem, w_xq, w_xk, w_xv, w_xo, g_final):
    p_lb = jax.nn.softmax(hg_lb.astype(jnp.float32), axis=0)
    lbs = jnp.cumsum(p_lb, axis=0) - p_lb[0]
    xp, xs = x_prompt, x_sample
    pk, pv, pki, phg, pmk, pmv = [], [], [], [], [], []
    sk, sv, ski, shg, sgv = [], [], [], [], []
    for l in range(DEPTH):
        pp = project(rms_norm(xp, g_mix[l]), w_in[l], lbs[l])
        s0 = jnp.zeros((xp.shape[0], HG_HEADS, HG_DIM, HG_DIM), jnp.float32)
        o_hg, s_fin = hgrn2_recurrence(pp['hg_q'], pp['hg_k'], pp['hg_v'], pp['hg_logf'], s0)
        gm_out, _ = gmlp_branch(pp['gm_u'], pp['gm_v'], g_gm[l], w_spatial[l], b_spatial[l])
        att = dsa_prompt(pp)
        xp = xp + merge(pp, o_hg, g_hg[l], gm_out, att, w_out[l])
        mk, mv = mem_kv(mem_prompt, g_mem[l], w_xk[l], w_xv[l])
        xp = xp + cross_attend(rms_norm(xp, g_xattn[l]), mk, mv, w_xq[l], w_xo[l])
        pk.append(pp['k']); pv.append(pp['v']); pki.append(pp['ki']); phg.append(s_fin); pmk.append(mk); pmv.append(mv)
        ps = project(rms_norm(xs, g_mix[l]), w_in[l], lbs[l])
        o_hg_s, s_new = hgrn2_recurrence(ps['hg_q'], ps['hg_k'], ps['hg_v'], ps['hg_logf'], state_hgrn[l])
        gm_out_s, vn_s = gmlp_branch(ps['gm_u'], ps['gm_v'], g_gm[l], w_spatial[l], b_spatial[l])
        att_s = dsa_sample(ps, cache_k, cache_v, cache_kidx, l, page_table)
        xs = xs + merge(ps, o_hg_s, g_hg[l], gm_out_s, att_s, w_out[l])
        xs = xs + cross_attend(rms_norm(xs, g_xattn[l]), cache_mem_k[l], cache_mem_v[l], w_xq[l], w_xo[l])
        sk.append(ps['k']); sv.append(ps['v']); ski.append(ps['ki']); shg.append(s_new); sgv.append(vn_s)
    y_prompt = rms_norm(xp, g_final)
    y_sample = rms_norm(xs, g_final)
    return (y_prompt, y_sample,
            jnp.stack(pk), jnp.stack(pv), jnp.stack(pki), jnp.stack(phg), jnp.stack(pmk), jnp.stack(pmv),
            jnp.stack(sk), jnp.stack(sv), jnp.stack(ski), jnp.stack(shg), jnp.stack(sgv))
```

```python
import functools
import math

import jax
import jax.numpy as jnp
from jax import lax
from jax.experimental import pallas as pl
from jax.experimental.pallas import tpu as pltpu

F32 = jnp.float32
BF16 = jnp.bfloat16

HG_HEADS = 4
HG_CHUNK = 64
GM_GROUPS = 4
GM_CHUNK = 128
ATT_HEADS = 8
ATT_DIM = 64
IDX_HEADS = 8
IDX_DIM = 64
TOPK_MAX = 256
PAGE_SIZE = 128
X_HEADS = 4
RMS_EPS = 1e-6

LANES = 128
VMEM_LIMIT_BYTES = 48 * 1024 * 1024

INT_MIN = -(2 ** 31)
NEG_BIG = -1e30


def _norm_matmul_kernel(*refs, norm, has_res):
    if has_res:
        x_ref, g_ref, w_ref, r_ref, o_ref, h_sc = refs
    else:
        x_ref, g_ref, w_ref, o_ref, h_sc = refs

    @pl.when(pl.program_id(1) == 0)
    def _():
        x = x_ref[...]
        if norm:
            ms = jnp.mean(x * x, axis=-1, keepdims=True)
            x = x * lax.rsqrt(ms + RMS_EPS) * g_ref[...]
        h_sc[...] = x.astype(BF16)

    acc = jnp.dot(h_sc[...], w_ref[...], preferred_element_type=F32)
    if has_res:
        acc = acc + r_ref[...]
    o_ref[...] = acc


def _norm_matmul(x, w, g=None, res=None, tm=512, tn=512):
    m, k = x.shape
    n = w.shape[1]
    tm = min(tm, m)
    tn = min(tn, n)
    assert m % tm == 0 and n % tn == 0
    norm = g is not None
    g2 = (g if norm else jnp.ones((k,), F32)).reshape(1, k).astype(F32)
    in_specs = [
        pl.BlockSpec((tm, k), lambda i, j: (i, 0)),
        pl.BlockSpec((1, k), lambda i, j: (0, 0)),
        pl.BlockSpec((k, tn), lambda i, j: (0, j)),
    ]
    args = [x, g2, w.astype(BF16)]
    if res is not None:
        in_specs.append(pl.BlockSpec((tm, tn), lambda i, j: (i, j)))
        args.append(res)
    return pl.pallas_call(
        functools.partial(_norm_matmul_kernel, norm=norm, has_res=res is not None),
        grid=(m // tm, n // tn),
        in_specs=in_specs,
        out_specs=pl.BlockSpec((tm, tn), lambda i, j: (i, j)),
        out_shape=jax.ShapeDtypeStruct((m, n), F32),
        scratch_shapes=[pltpu.VMEM((tm, k), BF16)],
        compiler_params=pltpu.CompilerParams(
            dimension_semantics=("arbitrary", "arbitrary"), vmem_limit_bytes=VMEM_LIMIT_BYTES),
    )(*args)


def _dsa_prompt_kernel(qT_ref, qiT_ref, wiT_ref, k_ref, vT_ref, ki_ref, o_ref, key_sc, bias_sc, *, topk):
    tq = LANES
    ck = LANES
    n_heads = k_ref.shape[1]
    dh = k_ref.shape[3]
    i = pl.program_id(1)
    n = i + 1
    row = lax.broadcasted_iota(jnp.int32, (ck, tq), 0)
    col = lax.broadcasted_iota(jnp.int32, (ck, tq), 1)

    def chunk(c):
        return pl.ds(pl.multiple_of(c * ck, ck), ck)

    def score_body(c, carry):
        s = jnp.dot(ki_ref[0, chunk(c), :], qiT_ref[0, 0], preferred_element_type=F32)
        acc = jnp.zeros((ck, tq), F32)
        for h in range(IDX_HEADS):
            acc = acc + jnp.maximum(s[:, h * tq:(h + 1) * tq], 0.0) * wiT_ref[0, 0, h:h + 1, :]
        bits = pltpu.bitcast(acc, jnp.int32)
        key = bits ^ ((bits >> 31) & 0x7FFFFFFF)
        key = jnp.where(key == -1, 0, key)
        adm = (row + c * ck) <= (col + i * tq)
        key_sc[chunk(c), :] = jnp.where(adm, key, INT_MIN)
        return carry

    lax.fori_loop(0, n, score_body, 0)

    def bit_body(bi, carry):
        thr, n_gt = carry
        cand = thr + lax.shift_left(jnp.int32(1), 31 - bi)

        def cnt_body(c, a):
            ge = jnp.where(key_sc[chunk(c), :] >= cand, 1.0, 0.0)
            return a + ge.reshape(ck // 8, 8, tq).sum(axis=0)

        a = lax.fori_loop(0, n, cnt_body, jnp.zeros((8, tq), F32))
        total = a.sum(axis=0, keepdims=True)
        ok = total >= topk
        return jnp.where(ok, cand, thr), jnp.where(ok, n_gt, total)

    thr, n_gt = lax.fori_loop(
        0, 32, bit_body, (jnp.full((1, tq), INT_MIN, jnp.int32), jnp.zeros((1, tq), F32)))

    need = topk - n_gt
    tri = jnp.where(row > col, 1.0, 0.0).astype(BF16)

    def bias_body(c, eq_before):
        kk = key_sc[chunk(c), :]
        eq = kk == thr
        eqf = jnp.where(eq, 1.0, 0.0)
        rank = jnp.dot(tri, eqf.astype(BF16), preferred_element_type=F32) + eq_before
        tie = jnp.where(rank < need, jnp.where(kk != INT_MIN, 0.0, NEG_BIG), NEG_BIG)
        bias_sc[chunk(c), :] = jnp.where(kk > thr, 0.0, jnp.where(eq, tie, NEG_BIG))
        return eq_before + eqf.sum(axis=0, keepdims=True)

    lax.fori_loop(0, n, bias_body, jnp.zeros((1, tq), F32))

    for h in range(n_heads):
        qT_h = qT_ref[0, 0, h]

        def att_body(c, carry, h=h, qT_h=qT_h):
            m, l, acc = carry
            s = jnp.dot(k_ref[0, h, chunk(c), :], qT_h, preferred_element_type=F32) + bias_sc[chunk(c), :]
            m_new = jnp.maximum(m, s.max(axis=0, keepdims=True))
            alpha = jnp.exp(m - m_new)
            p = jnp.exp(s - m_new)
            l = alpha * l + p.sum(axis=0, keepdims=True)
            acc = alpha * acc + jnp.dot(vT_ref[0, h, c], p.astype(BF16), preferred_element_type=F32)
            return m_new, l, acc

        m, l, acc = lax.fori_loop(
            0, n, att_body,
            (jnp.full((1, tq), NEG_BIG, F32), jnp.zeros((1, tq), F32), jnp.zeros((dh, tq), F32)))
        o_ref[0, 0, h * dh:(h + 1) * dh, :] = acc / l


def _dsa_prompt(q, k, v, qi, ki, wi):
    bn, length, nh, dh = q.shape
    tq = LANES
    nq = length // tq
    topk = min(TOPK_MAX, length // 4)
    scale = ATT_DIM ** -0.5
    qT = (q * scale).astype(BF16).reshape(bn, nq, tq, nh, dh).transpose(0, 1, 3, 4, 2)
    qiT = qi.astype(BF16).reshape(bn, nq, tq, IDX_HEADS, IDX_DIM).transpose(0, 1, 4, 3, 2)
    qiT = qiT.reshape(bn, nq, IDX_DIM, IDX_HEADS * tq)
    wiT = wi.astype(F32).reshape(bn, nq, tq, IDX_HEADS).transpose(0, 1, 3, 2)
    k_hm = k.astype(BF16).transpose(0, 2, 1, 3)
    vT = v.astype(BF16).reshape(bn, nq, tq, nh, dh).transpose(0, 3, 1, 4, 2)
    ki_b = ki.astype(BF16)
    oT = pl.pallas_call(
        functools.partial(_dsa_prompt_kernel, topk=float(topk)),
        grid=(bn, nq),
        in_specs=[
            pl.BlockSpec((1, 1, nh, dh, tq), lambda b, i: (b, i, 0, 0, 0)),
            pl.BlockSpec((1, 1, IDX_DIM, IDX_HEADS * tq), lambda b, i: (b, i, 0, 0)),
            pl.BlockSpec((1, 1, IDX_HEADS, tq), lambda b, i: (b, i, 0, 0)),
            pl.BlockSpec((1, nh, length, dh), lambda b, i: (b, 0, 0, 0)),
            pl.BlockSpec((1, nh, nq, dh, tq), lambda b, i: (b, 0, 0, 0, 0)),
            pl.BlockSpec((1, length, IDX_DIM), lambda b, i: (b, 0, 0)),
        ],
        out_specs=pl.BlockSpec((1, 1, nh * dh, tq), lambda b, i: (b, i, 0, 0)),
        out_shape=jax.ShapeDtypeStruct((bn, nq, nh * dh, tq), F32),
        scratch_shapes=[pltpu.VMEM((length, tq), jnp.int32), pltpu.VMEM((length, tq), F32)],
        compiler_params=pltpu.CompilerParams(
            dimension_semantics=("arbitrary", "arbitrary"), vmem_limit_bytes=VMEM_LIMIT_BYTES),
    )(qT, qiT, wiT, k_hm, vT, ki_b)
    return oT.transpose(0, 1, 3, 2).reshape(bn, length, nh, dh)


def _rms_norm(x, g):
    xf = x.astype(F32)
    y = xf * lax.rsqrt(jnp.mean(xf * xf, axis=-1, keepdims=True) + RMS_EPS)
    return (y * g.astype(F32)).astype(x.dtype)


def _project(x, g, w_in, lb, splits):
    bn, length, d = x.shape
    n_in = w_in.shape[1]
    n_pad = -(-n_in // 512) * 512
    w_pad = jnp.pad(w_in, ((0, 0), (0, n_pad - n_in)))
    z = _norm_matmul(x.reshape(bn * length, d), w_pad, g=g).reshape(bn, length, n_pad)
    cuts = []
    acc = 0
    for s in splits:
        cuts.append((acc, acc + s))
        acc += s
    (a_q, a_f, a_i, a_g, b_u, b_v, b_g, c_q, c_k, c_v, c_g, c_qi, c_ki, c_wi) = [z[..., a:b] for a, b in cuts]
    f = lb + (1.0 - lb) * jax.nn.sigmoid(a_f)
    heads = lambda t, nh: t.reshape(bn, length, nh, -1)
    return {
        'hg_q': heads(a_q, HG_HEADS), 'hg_k': heads(1.0 - f, HG_HEADS), 'hg_v': heads(a_i, HG_HEADS),
        'hg_logf': heads(jnp.log(f), HG_HEADS), 'hg_gate': a_g,
        'gm_u': jax.nn.gelu(b_u), 'gm_v': jax.nn.gelu(b_v), 'gm_gate': b_g,
        'q': heads(c_q, ATT_HEADS), 'k': heads(c_k, ATT_HEADS), 'v': heads(c_v, ATT_HEADS), 'att_gate': c_g,
        'qi': heads(c_qi, IDX_HEADS), 'ki': c_ki, 'wi': c_wi,
    }


def _hgrn2_recurrence(q, k, v, logf, s0):
    bn, length, nh, _ = q.shape
    dv = v.shape[-1]
    c = math.gcd(length, HG_CHUNK)
    nc = length // c

    def chunks(a):
        return a.astype(F32).reshape(bn, nc, c, nh, a.shape[-1]).transpose(1, 0, 3, 2, 4)

    causal = jnp.tril(jnp.ones((c, c), dtype=bool))

    def step(s, inp):
        qc, kc, vc, gc = inp
        b = jnp.cumsum(gc, axis=2)
        o_inter = jnp.einsum('bhtk,bhkv->bhtv', qc * jnp.exp(b), s)
        diff = b[:, :, :, None, :] - b[:, :, None, :, :]
        decay = jnp.exp(jnp.where(causal[:, :, None], diff, -jnp.inf))
        att = jnp.einsum('bhtk,bhtsk,bhsk->bhts', qc, decay, kc)
        o = o_inter + jnp.einsum('bhts,bhsv->bhtv', att, vc)
        b_last = b[:, :, -1:, :]
        s = jnp.exp(b_last[:, :, 0, :])[..., None] * s + jnp.einsum('bhsk,bhsv->bhkv', kc * jnp.exp(b_last - b), vc)
        return s, o

    s_fin, o = lax.scan(step, s0.astype(F32), (chunks(q), chunks(k), chunks(v), chunks(logf)))
    o = o.transpose(1, 0, 3, 2, 4).reshape(bn, length, nh, dv)
    return o.astype(v.dtype), s_fin


def _gmlp_branch(u, v, g_v, w_s, b_s):
    bn, length, width = u.shape
    dg = width // GM_GROUPS
    vn = _rms_norm(v, g_v)
    padded = -(-length // GM_CHUNK) * GM_CHUNK
    vp = jnp.pad(vn.reshape(bn, length, GM_GROUPS, dg), ((0, 0), (0, padded - length), (0, 0), (0, 0)))
    vc = vp.reshape(bn, padded // GM_CHUNK, GM_CHUNK, GM_GROUPS, dg)
    w_causal = w_s * jnp.tril(jnp.ones((GM_CHUNK, GM_CHUNK), w_s.dtype))
    mixed = jnp.einsum('gts,bcsgd->bctgd', w_causal, vc) + b_s.T[None, None, :, :, None]
    mixed = mixed.reshape(bn, padded, GM_GROUPS, dg)[:, :length]
    return u * mixed.reshape(bn, length, width), vn


def _gather_rows(a, idx):
    return jax.vmap(lambda ar, ir: ar[ir])(a, idx)


def _dsa_select(qi, wi, ki, q_pos, k_pos, k_top):
    s = jnp.einsum('bthd,bsd->bths', qi, ki) * (IDX_DIM ** -0.5)
    score = jnp.einsum('bths,bth->bts', jax.nn.relu(s), wi) * (IDX_HEADS ** -0.5)
    allowed = k_pos[None, :] <= q_pos[:, None]
    score = jnp.where(allowed[None], score, -jnp.inf)
    _, idx = lax.top_k(score, k_top)
    valid = idx <= q_pos[None, :, None]
    return idx, valid


def _sparse_attend(q, k_sel, v_sel, valid):
    logits = jnp.einsum('bthd,btkhd->bthk', q, k_sel) * (ATT_DIM ** -0.5)
    logits = jnp.where(valid[:, :, None, :], logits, -jnp.inf)
    p = jax.nn.softmax(logits, axis=-1)
    return jnp.einsum('bthk,btkhd->bthd', p, v_sel)


def _dsa_sample(p, cache_k, cache_v, cache_kidx, layer, page_table):
    bn, n = p['q'].shape[:2]
    past = page_table.shape[1] * PAGE_SIZE
    ki_past = cache_kidx[layer, page_table].reshape(bn, past, IDX_DIM)
    ki_all = jnp.concatenate([ki_past, p['ki']], axis=1)
    n_keys = past + n
    k_top = min(TOPK_MAX, n_keys // 4)
    idx, valid = _dsa_select(p['qi'], p['wi'], ki_all, past + jnp.arange(n), jnp.arange(n_keys), k_top)
    is_past = idx < past
    pidx = jnp.minimum(idx, past - 1)
    phys = jnp.take_along_axis(page_table, (pidx // PAGE_SIZE).reshape(bn, -1), axis=1).reshape(idx.shape)
    off = pidx % PAGE_SIZE
    nidx = jnp.clip(idx - past, 0, n - 1)

    def pick(pool, new):
        return jnp.where(is_past[..., None, None], pool[layer, phys, off], _gather_rows(new, nidx))

    return _sparse_attend(p['q'], pick(cache_k, p['k']), pick(cache_v, p['v']), valid)


def _merge(x, p, o_hg, g_hg, gm_out, att_out, w_out):
    bn, length, d = x.shape
    hg_dim = o_hg.shape[-1]
    y_hg = _rms_norm(o_hg, g_hg.reshape(HG_HEADS, hg_dim)).reshape(bn, length, HG_HEADS * hg_dim)
    cat = jnp.concatenate([
        y_hg * jax.nn.silu(p['hg_gate']),
        gm_out * jax.nn.silu(p['gm_gate']),
        att_out.reshape(bn, length, -1) * jax.nn.silu(p['att_gate']),
    ], axis=-1)
    m = bn * length
    return _norm_matmul(cat.reshape(m, -1), w_out, res=x.reshape(m, d)).reshape(bn, length, d)


def _cross_attend(x, g, mk, mv, w_q, w_o):
    bn, length, d = x.shape
    m = bn * length
    xw = w_q.shape[1]
    xd = xw // X_HEADS
    q = _norm_matmul(x.reshape(m, d), w_q, g=g).reshape(bn, length, X_HEADS, xd)
    logits = jnp.einsum('blhd,bmhd->bhlm', q, mk) * (xd ** -0.5)
    pr = jax.nn.softmax(logits, axis=-1)
    o = jnp.einsum('bhlm,bmhd->blhd', pr, mv)
    return _norm_matmul(o.reshape(m, xw), w_o, res=x.reshape(m, d)).reshape(bn, length, d)


def kernel(x_prompt, x_sample, cache_k, cache_v, cache_kidx, cache_mem_k, cache_mem_v, state_hgrn, page_table, mem_prompt,
           g_mix, w_in, hg_lb, g_hg, g_gm, w_spatial, b_spatial, w_out, g_xattn, g_mem, w_xq, w_xk, w_xv, w_xo, g_final):
    depth, d_model, _ = w_in.shape
    mix = w_out.shape[1]
    hgw, gmw, attw = mix // 4, mix // 4, mix // 2
    splits = (hgw,) * 4 + (gmw,) * 3 + (attw,) * 4 + (IDX_HEADS * IDX_DIM, IDX_DIM, IDX_HEADS)
    hg_dim = hgw // HG_HEADS
    p_lb = jax.nn.softmax(hg_lb.astype(F32), axis=0)
    lbs = jnp.cumsum(p_lb, axis=0) - p_lb[0]
    xp, xs = x_prompt, x_sample
    bp, mem_len, _ = mem_prompt.shape
    pk, pv, pki, phg, pmk, pmv = [], [], [], [], [], []
    sk, sv, ski, shg, sgv = [], [], [], [], []
    for l in range(depth):
        pp = _project(xp, g_mix[l], w_in[l], lbs[l], splits)
        s0 = jnp.zeros((xp.shape[0], HG_HEADS, hg_dim, hg_dim), F32)
        o_hg, s_fin = _hgrn2_recurrence(pp['hg_q'], pp['hg_k'], pp['hg_v'], pp['hg_logf'], s0)
        gm_out, _ = _gmlp_branch(pp['gm_u'], pp['gm_v'], g_gm[l], w_spatial[l], b_spatial[l])
        att = _dsa_prompt(pp['q'], pp['k'], pp['v'], pp['qi'], pp['ki'], pp['wi'])
        xp = _merge(xp, pp, o_hg, g_hg[l], gm_out, att, w_out[l])
        mem2 = mem_prompt.reshape(bp * mem_len, d_model)
        xw = w_xk.shape[2]
        mk = _norm_matmul(mem2, w_xk[l], g=g_mem[l]).reshape(bp, mem_len, X_HEADS, xw // X_HEADS)
        mv = _norm_matmul(mem2, w_xv[l], g=g_mem[l]).reshape(bp, mem_len, X_HEADS, xw // X_HEADS)
        xp = _cross_attend(xp, g_xattn[l], mk, mv, w_xq[l], w_xo[l])
        pk.append(pp['k']); pv.append(pp['v']); pki.append(pp['ki']); phg.append(s_fin); pmk.append(mk); pmv.append(mv)

        ps = _project(xs, g_mix[l], w_in[l], lbs[l], splits)
        o_hg_s, s_new = _hgrn2_recurrence(ps['hg_q'], ps['hg_k'], ps['hg_v'], ps['hg_logf'], state_hgrn[l])
        gm_out_s, vn_s = _gmlp_branch(ps['gm_u'], ps['gm_v'], g_gm[l], w_spatial[l], b_spatial[l])
        att_s = _dsa_sample(ps, cache_k, cache_v, cache_kidx, l, page_table)
        xs = _merge(xs, ps, o_hg_s, g_hg[l], gm_out_s, att_s, w_out[l])
        xs = _cross_attend(xs, g_xattn[l], cache_mem_k[l], cache_mem_v[l], w_xq[l], w_xo[l])
        sk.append(ps['k']); sv.append(ps['v']); ski.append(ps['ki']); shg.append(s_new); sgv.append(vn_s)
    y_prompt = _rms_norm(xp, g_final)
    y_sample = _rms_norm(xs, g_final)
    return (y_prompt, y_sample,
            jnp.stack(pk), jnp.stack(pv), jnp.stack(pki), jnp.stack(phg), jnp.stack(pmk), jnp.stack(pmv),
            jnp.stack(sk), jnp.stack(sv), jnp.stack(ski), jnp.stack(shg), jnp.stack(sgv))
```

```python
import functools
import math

import jax
import jax.numpy as jnp
from jax import lax
from jax.experimental import pallas as pl
from jax.experimental.pallas import tpu as pltpu

F32 = jnp.float32
BF16 = jnp.bfloat16

HG_HEADS = 4
HG_CHUNK = 64
GM_GROUPS = 4
GM_CHUNK = 128
ATT_HEADS = 8
ATT_DIM = 64
IDX_HEADS = 8
IDX_DIM = 64
TOPK_MAX = 256
PAGE_SIZE = 128
X_HEADS = 4
RMS_EPS = 1e-6

LANES = 128
VMEM_LIMIT_BYTES = 48 * 1024 * 1024

INT_MIN = -(2 ** 31)
NEG_BIG = -1e30


def _norm_matmul_kernel(*refs, norm, has_res):
    if has_res:
        x_ref, g_ref, w_ref, r_ref, o_ref, h_sc = refs
    else:
        x_ref, g_ref, w_ref, o_ref, h_sc = refs

    @pl.when(pl.program_id(1) == 0)
    def _():
        x = x_ref[...]
        if norm:
            ms = jnp.mean(x * x, axis=-1, keepdims=True)
            x = x * lax.rsqrt(ms + RMS_EPS) * g_ref[...]
        h_sc[...] = x.astype(BF16)

    acc = jnp.dot(h_sc[...], w_ref[...], preferred_element_type=F32)
    if has_res:
        acc = acc + r_ref[...]
    o_ref[...] = acc


def _norm_matmul(x, w, g=None, res=None, tm=512, tn=512):
    m, k = x.shape
    n = w.shape[1]
    tm = min(tm, m)
    tn = min(tn, n)
    assert m % tm == 0 and n % tn == 0
    norm = g is not None
    g2 = (g if norm else jnp.ones((k,), F32)).reshape(1, k).astype(F32)
    in_specs = [
        pl.BlockSpec((tm, k), lambda i, j: (i, 0)),
        pl.BlockSpec((1, k), lambda i, j: (0, 0)),
        pl.BlockSpec((k, tn), lambda i, j: (0, j)),
    ]
    args = [x, g2, w.astype(BF16)]
    if res is not None:
        in_specs.append(pl.BlockSpec((tm, tn), lambda i, j: (i, j)))
        args.append(res)
    return pl.pallas_call(
        functools.partial(_norm_matmul_kernel, norm=norm, has_res=res is not None),
        grid=(m // tm, n // tn),
        in_specs=in_specs,
        out_specs=pl.BlockSpec((tm, tn), lambda i, j: (i, j)),
        out_shape=jax.ShapeDtypeStruct((m, n), F32),
        scratch_shapes=[pltpu.VMEM((tm, k), BF16)],
        compiler_params=pltpu.CompilerParams(
            dimension_semantics=("arbitrary", "arbitrary"), vmem_limit_bytes=VMEM_LIMIT_BYTES),
    )(*args)


def _dsa_prompt_kernel(qT_ref, qiT_ref, wiT_ref, k_ref, vT_ref, ki_ref, o_ref,
                       key_sc, bias_sc, m_sc, l_sc, acc_sc, *, topk):
    tq = qT_ref.shape[4] // 2
    ck = tq
    n_pairs = k_ref.shape[1]
    dh = vT_ref.shape[3]
    i = pl.program_id(1)
    n = i + 1
    row = lax.broadcasted_iota(jnp.int32, (ck, tq), 0)
    col = lax.broadcasted_iota(jnp.int32, (ck, tq), 1)

    def chunk(c):
        return pl.ds(pl.multiple_of(c * ck, ck), ck)

    def score_body(c, carry):
        ki_c = ki_ref[0, chunk(c), :]
        acc = jnp.zeros((ck, tq), F32)
        for h in range(IDX_HEADS):
            s = jnp.dot(ki_c, qiT_ref[0, 0, :, h * tq:(h + 1) * tq], preferred_element_type=F32)
            acc = acc + jnp.maximum(s, 0.0) * wiT_ref[0, 0, h:h + 1, :]
        bits = pltpu.bitcast(acc, jnp.int32)
        key = bits ^ ((bits >> 31) & 0x7FFFFFFF)
        key = jnp.where(key == -1, 0, key)
        adm = (row + c * ck) <= (col + i * tq)
        key_sc[chunk(c), :] = jnp.where(adm, key, INT_MIN)
        return carry

    lax.fori_loop(0, n, score_body, 0)

    def bit_body(bi, carry):
        thr, n_gt = carry
        cand = thr + lax.shift_left(jnp.int32(1), 31 - bi)

        def cnt_body(c, a):
            ge = jnp.where(key_sc[chunk(c), :] >= cand, 1.0, 0.0)
            part = ge.reshape(4, ck // 4, tq).sum(axis=0)
            return a + part.reshape(ck // 32, 8, tq).sum(axis=0)

        a = lax.fori_loop(0, n, cnt_body, jnp.zeros((8, tq), F32))
        total = a.sum(axis=0, keepdims=True)
        ok = total >= topk
        return jnp.where(ok, cand, thr), jnp.where(ok, n_gt, total)

    thr, n_gt = lax.fori_loop(
        0, 32, bit_body, (jnp.full((1, tq), INT_MIN, jnp.int32), jnp.zeros((1, tq), F32)))

    need = topk - n_gt
    tri = jnp.where(row > col, 1.0, 0.0).astype(BF16)

    def bias_body(c, eq_before):
        kk = key_sc[chunk(c), :]
        eq = kk == thr
        eqf = jnp.where(eq, 1.0, 0.0)
        rank = jnp.dot(tri, eqf.astype(BF16), preferred_element_type=F32) + eq_before
        tie = jnp.where(rank < need, jnp.where(kk != INT_MIN, 0.0, NEG_BIG), NEG_BIG)
        bias_sc[chunk(c), :] = jnp.where(kk > thr, 0.0, jnp.where(eq, tie, NEG_BIG))
        return eq_before + eqf.sum(axis=0, keepdims=True)

    lax.fori_loop(0, n, bias_body, jnp.zeros((1, tq), F32))

    m_sc[...] = jnp.full(m_sc.shape, NEG_BIG, F32)
    l_sc[...] = jnp.zeros(l_sc.shape, F32)
    acc_sc[...] = jnp.zeros(acc_sc.shape, F32)

    def att_body(c, carry):
        bias = bias_sc[chunk(c), :]
        for pr in range(n_pairs):
            s2 = jnp.dot(k_ref[0, pr, chunk(c), :], qT_ref[0, 0, pr], preferred_element_type=F32)
            for e in range(2):
                h = 2 * pr + e
                s = s2[:, e * tq:(e + 1) * tq] + bias
                m_old = m_sc[h:h + 1, :]
                m_new = jnp.maximum(m_old, s.max(axis=0, keepdims=True))
                alpha = jnp.exp2(m_old - m_new)
                p = jnp.exp2(s - m_new)
                l_sc[h:h + 1, :] = alpha * l_sc[h:h + 1, :] + p.sum(axis=0, keepdims=True)
                m_sc[h:h + 1, :] = m_new
                pv = jnp.dot(vT_ref[0, h, c], p.astype(BF16), preferred_element_type=F32)
                acc_sc[h] = alpha * acc_sc[h] + pv
        return carry

    lax.fori_loop(0, n, att_body, 0)
    for h in range(2 * n_pairs):
        o_ref[0, 0, h * dh:(h + 1) * dh, :] = acc_sc[h] / l_sc[h:h + 1, :]


def _dsa_prompt(q, k, v, qi, ki, wi, tq=256):
    bn, length, nh, dh = q.shape
    tq = min(tq, length)
    nq = length // tq
    topk = min(TOPK_MAX, length // 4)
    scale = (ATT_DIM ** -0.5) * math.log2(math.e)
    qT = (q * scale).astype(BF16).reshape(bn, nq, tq, nh // 2, 2, dh).transpose(0, 1, 3, 4, 5, 2)
    eye2 = jnp.eye(2, dtype=BF16)
    qT_bd = jnp.einsum('bnpedt,ef->bnpedft', qT, eye2).reshape(bn, nq, nh // 2, 2 * dh, 2 * tq)
    qiT = qi.astype(BF16).reshape(bn, nq, tq, IDX_HEADS, IDX_DIM).transpose(0, 1, 4, 3, 2)
    qiT = qiT.reshape(bn, nq, IDX_DIM, IDX_HEADS * tq)
    wiT = wi.astype(F32).reshape(bn, nq, tq, IDX_HEADS).transpose(0, 1, 3, 2)
    k_pairs = k.astype(BF16).reshape(bn, length, nh // 2, 2 * dh).transpose(0, 2, 1, 3)
    vT = v.astype(BF16).reshape(bn, nq, tq, nh, dh).transpose(0, 3, 1, 4, 2)
    ki_b = ki.astype(BF16)
    oT = pl.pallas_call(
        functools.partial(_dsa_prompt_kernel, topk=float(topk)),
        grid=(bn, nq),
        in_specs=[
            pl.BlockSpec((1, 1, nh // 2, 2 * dh, 2 * tq), lambda b, i: (b, i, 0, 0, 0)),
            pl.BlockSpec((1, 1, IDX_DIM, IDX_HEADS * tq), lambda b, i: (b, i, 0, 0)),
            pl.BlockSpec((1, 1, IDX_HEADS, tq), lambda b, i: (b, i, 0, 0)),
            pl.BlockSpec((1, nh // 2, length, 2 * dh), lambda b, i: (b, 0, 0, 0)),
            pl.BlockSpec((1, nh, nq, dh, tq), lambda b, i: (b, 0, 0, 0, 0)),
            pl.BlockSpec((1, length, IDX_DIM), lambda b, i: (b, 0, 0)),
        ],
        out_specs=pl.BlockSpec((1, 1, nh * dh, tq), lambda b, i: (b, i, 0, 0)),
        out_shape=jax.ShapeDtypeStruct((bn, nq, nh * dh, tq), F32),
        scratch_shapes=[
            pltpu.VMEM((length, tq), jnp.int32), pltpu.VMEM((length, tq), F32),
            pltpu.VMEM((nh, tq), F32), pltpu.VMEM((nh, tq), F32), pltpu.VMEM((nh, dh, tq), F32)],
        compiler_params=pltpu.CompilerParams(
            dimension_semantics=("arbitrary", "arbitrary"), vmem_limit_bytes=VMEM_LIMIT_BYTES),
    )(qT_bd, qiT, wiT, k_pairs, vT, ki_b)
    return oT.transpose(0, 1, 3, 2).reshape(bn, length, nh, dh)


QPAD = 8


def _order_key(score):
    bits = pltpu.bitcast(score, jnp.int32)
    key = bits ^ ((bits >> 31) & 0x7FFFFFFF)
    return jnp.where(key == -1, 0, key)


def _sample_index_kernel(pt_ref, qi_ref, wcol_ref, kin_ref, *rest, pp, n_new, topk):
    page_refs = rest[:pp]
    bias_ref, key_sc = rest[pp], rest[pp + 1]
    g = pl.program_id(1)
    n_pages = pl.num_programs(1) * pp
    qi = qi_ref[0]
    wcol = wcol_ref[0]

    def page_keys(page):
        s = lax.dot_general(qi, page.astype(BF16), (((1,), (1,)), ((), ())), preferred_element_type=F32)
        sc = (jnp.maximum(s, 0.0) * wcol).reshape(IDX_HEADS, QPAD, PAGE_SIZE).sum(axis=0)
        return _order_key(sc)

    for j in range(pp):
        key_sc[g * pp + j] = page_keys(page_refs[j][0, 0])

    @pl.when(g == pl.num_programs(1) - 1)
    def _():
        rowq = lax.broadcasted_iota(jnp.int32, (QPAD, PAGE_SIZE), 0)
        lane = lax.broadcasted_iota(jnp.int32, (QPAD, PAGE_SIZE), 1)
        adm_new = (lane <= rowq) & (lane < n_new)
        key_sc[n_pages] = jnp.where(adm_new, page_keys(kin_ref[0]), INT_MIN)
        n_all = n_pages + 1

        def count_ge(cand):
            cand_b = jnp.broadcast_to(cand, (QPAD, PAGE_SIZE))
            a = lax.fori_loop(
                0, n_all, lambda p, a: a + jnp.where(key_sc[p] >= cand_b, 1.0, 0.0),
                jnp.zeros((QPAD, PAGE_SIZE), F32), unroll=4)
            return a.sum(axis=1, keepdims=True)

        def bit_body(bi, carry):
            thr, n_gt = carry
            cand = thr + lax.shift_left(jnp.int32(1), 31 - bi)
            total = count_ge(cand)
            ok = total >= topk
            return jnp.where(ok, cand, thr), jnp.where(ok, n_gt, total)

        thr, n_gt = lax.fori_loop(
            0, 32, bit_body, (jnp.full((QPAD, 1), INT_MIN, jnp.int32), jnp.zeros((QPAD, 1), F32)))
        n_ge = count_ge(thr)
        thr_b = jnp.broadcast_to(thr, (QPAD, PAGE_SIZE))

        def fast_body(p, carry):
            kk = key_sc[p]
            bias_ref[0, p] = jnp.where(kk >= thr_b, jnp.where(kk != INT_MIN, 0.0, NEG_BIG), NEG_BIG)
            return carry

        lax.fori_loop(0, n_all, fast_body, 0)

        @pl.when(jnp.max(n_ge) > topk)
        def _():
            need = jnp.broadcast_to(topk - n_gt, (QPAD, PAGE_SIZE))
            r = lax.broadcasted_iota(jnp.int32, (PAGE_SIZE, PAGE_SIZE), 0)
            c = lax.broadcasted_iota(jnp.int32, (PAGE_SIZE, PAGE_SIZE), 1)
            before = jnp.where(r < c, 1.0, 0.0).astype(BF16)

            def tie_body(p, eq_before):
                kk = key_sc[p]
                eq = kk == thr_b
                eqf = jnp.where(eq, 1.0, 0.0)
                rank = jnp.dot(eqf.astype(BF16), before, preferred_element_type=F32) + eq_before
                tie = jnp.where(rank < need, jnp.where(kk != INT_MIN, 0.0, NEG_BIG), NEG_BIG)
                bias_ref[0, p] = jnp.where(kk > thr_b, 0.0, jnp.where(eq, tie, NEG_BIG))
                return eq_before + eqf.sum(axis=1, keepdims=True)

            lax.fori_loop(0, n_all, tie_body, jnp.zeros((QPAD, 1), F32))


def _sample_attn_kernel(pt_ref, q_ref, bias_ref, biasn_ref, kn_ref, vn_ref, *rest, pp):
    k_refs, v_refs = rest[:pp], rest[pp:2 * pp]
    o_ref, m_sc, l_sc, acc_sc = rest[2 * pp:]
    g = pl.program_id(1)
    qbd = q_ref[0]

    @pl.when(g == 0)
    def _():
        m_sc[...] = jnp.full(m_sc.shape, NEG_BIG, F32)
        l_sc[...] = jnp.zeros(l_sc.shape, F32)
        acc_sc[...] = jnp.zeros(acc_sc.shape, F32)

    def update(ks, vs, biases):
        width = PAGE_SIZE * len(ks)
        logits = jnp.concatenate(
            [lax.dot_general(qbd, kp.astype(BF16), (((1,), (1,)), ((), ())), preferred_element_type=F32) for kp in ks],
            axis=1)
        bias = jnp.concatenate(biases, axis=1)
        s = (logits.reshape(ATT_HEADS, QPAD, width) + bias[None]).reshape(ATT_HEADS * QPAD, width)
        m_old = m_sc[...]
        m_new = jnp.maximum(m_old, s.max(axis=1, keepdims=True))
        alpha = jnp.exp2(m_old - m_new)
        p = jnp.exp2(s - m_new)
        l_sc[...] = alpha * l_sc[...] + p.sum(axis=1, keepdims=True)
        m_sc[...] = m_new
        pv = jnp.zeros(acc_sc.shape, F32)
        for j, vp in enumerate(vs):
            pv = pv + jnp.dot(p[:, j * PAGE_SIZE:(j + 1) * PAGE_SIZE].astype(BF16), vp.astype(BF16),
                              preferred_element_type=F32)
        acc_sc[...] = alpha * acc_sc[...] + pv

    update([r[0, 0] for r in k_refs], [r[0, 0] for r in v_refs], [bias_ref[0, j] for j in range(pp)])

    @pl.when(g == pl.num_programs(1) - 1)
    def _():
        update([kn_ref[0]], [vn_ref[0]], [biasn_ref[0, 0]])
        o_ref[0] = acc_sc[...] / l_sc[...]


def _dsa_sample(q, k, v, qi, ki, wi, cache_k, cache_v, cache_kidx, layer, page_table, pp_idx=16, pp_att=4):
    bn, n_new, nh, dh = q.shape
    n_pages = page_table.shape[1]
    depth, n_pool = cache_k.shape[:2]
    topk = min(TOPK_MAX, (n_pages * PAGE_SIZE + n_new) // 4)
    pp_idx = math.gcd(pp_idx, n_pages)
    pp_att = math.gcd(pp_att, n_pages)
    pt_flat = page_table.reshape(-1).astype(jnp.int32)
    padq = lambda a: jnp.pad(a, ((0, 0), (0, 0), (0, QPAD - n_new)) + ((0, 0),) * (a.ndim - 3))
    rows_i = IDX_HEADS * QPAD
    qi_rows = padq(qi.astype(BF16).transpose(0, 2, 1, 3)).reshape(bn, rows_i, IDX_DIM)
    wcol = jnp.broadcast_to(padq(wi.astype(F32).transpose(0, 2, 1)).reshape(bn, rows_i, 1), (bn, rows_i, PAGE_SIZE))
    padk = lambda a: jnp.pad(a.astype(F32), ((0, 0), (0, PAGE_SIZE - n_new), (0, 0)))
    ki_new = padk(ki)

    def page_spec_idx(j):
        return pl.BlockSpec((1, 1, PAGE_SIZE, IDX_DIM),
                            lambda b, g, pt, j=j: (layer, pt[b * n_pages + g * pp_idx + j], 0, 0))

    bias = pl.pallas_call(
        functools.partial(_sample_index_kernel, pp=pp_idx, n_new=n_new, topk=float(topk)),
        grid_spec=pltpu.PrefetchScalarGridSpec(
            num_scalar_prefetch=1,
            grid=(bn, n_pages // pp_idx),
            in_specs=[
                pl.BlockSpec((1, rows_i, IDX_DIM), lambda b, g, pt: (b, 0, 0)),
                pl.BlockSpec((1, rows_i, PAGE_SIZE), lambda b, g, pt: (b, 0, 0)),
                pl.BlockSpec((1, PAGE_SIZE, IDX_DIM), lambda b, g, pt: (b, 0, 0)),
            ] + [page_spec_idx(j) for j in range(pp_idx)],
            out_specs=pl.BlockSpec((1, n_pages + 1, QPAD, PAGE_SIZE), lambda b, g, pt: (b, 0, 0, 0)),
            scratch_shapes=[pltpu.VMEM((n_pages + 1, QPAD, PAGE_SIZE), jnp.int32)],
        ),
        out_shape=jax.ShapeDtypeStruct((bn, n_pages + 1, QPAD, PAGE_SIZE), F32),
        compiler_params=pltpu.CompilerParams(
            dimension_semantics=("arbitrary", "arbitrary"), vmem_limit_bytes=VMEM_LIMIT_BYTES),
    )(pt_flat, qi_rows, wcol, ki_new, *([cache_kidx] * pp_idx))

    width = nh * dh
    rows_a = nh * QPAD
    scale = (ATT_DIM ** -0.5) * math.log2(math.e)
    q_rows = padq((q * scale).astype(BF16).transpose(0, 2, 1, 3))
    q_bd = jnp.einsum('bhjd,hg->bhjgd', q_rows, jnp.eye(nh, dtype=BF16)).reshape(bn, rows_a, width)
    k_new = padk(k.reshape(bn, n_new, width))
    v_new = padk(v.reshape(bn, n_new, width))
    ck = cache_k.reshape(depth, n_pool, PAGE_SIZE, width)
    cv = cache_v.reshape(depth, n_pool, PAGE_SIZE, width)

    def page_spec_att(j):
        return pl.BlockSpec((1, 1, PAGE_SIZE, width),
                            lambda b, g, pt, j=j: (layer, pt[b * n_pages + g * pp_att + j], 0, 0))

    o_rows = pl.pallas_call(
        functools.partial(_sample_attn_kernel, pp=pp_att),
        grid_spec=pltpu.PrefetchScalarGridSpec(
            num_scalar_prefetch=1,
            grid=(bn, n_pages // pp_att),
            in_specs=[
                pl.BlockSpec((1, rows_a, width), lambda b, g, pt: (b, 0, 0)),
                pl.BlockSpec((1, pp_att, QPAD, PAGE_SIZE), lambda b, g, pt: (b, g, 0, 0)),
                pl.BlockSpec((1, 1, QPAD, PAGE_SIZE), lambda b, g, pt: (b, n_pages, 0, 0)),
                pl.BlockSpec((1, PAGE_SIZE, width), lambda b, g, pt: (b, 0, 0)),
                pl.BlockSpec((1, PAGE_SIZE, width), lambda b, g, pt: (b, 0, 0)),
            ] + [page_spec_att(j) for j in range(pp_att)] * 2,
            out_specs=pl.BlockSpec((1, rows_a, width), lambda b, g, pt: (b, 0, 0)),
            scratch_shapes=[pltpu.VMEM((rows_a, 1), F32), pltpu.VMEM((rows_a, 1), F32),
                            pltpu.VMEM((rows_a, width), F32)],
        ),
        out_shape=jax.ShapeDtypeStruct((bn, rows_a, width), F32),
        compiler_params=pltpu.CompilerParams(
            dimension_semantics=("arbitrary", "arbitrary"), vmem_limit_bytes=VMEM_LIMIT_BYTES),
    )(pt_flat, q_bd, bias, bias, k_new, v_new, *([ck] * pp_att), *([cv] * pp_att))
    o = o_rows.reshape(bn, nh, QPAD, nh, dh)[:, :, :n_new]
    return jnp.einsum('bhjgd,hg->bjhd', o, jnp.eye(nh, dtype=F32))


def _rms_norm(x, g):
    xf = x.astype(F32)
    y = xf * lax.rsqrt(jnp.mean(xf * xf, axis=-1, keepdims=True) + RMS_EPS)
    return (y * g.astype(F32)).astype(x.dtype)


def _project(x, g, w_in, lb, splits):
    bn, length, d = x.shape
    n_in = w_in.shape[1]
    n_pad = -(-n_in // 512) * 512
    w_pad = jnp.pad(w_in, ((0, 0), (0, n_pad - n_in)))
    z = _norm_matmul(x.reshape(bn * length, d), w_pad, g=g).reshape(bn, length, n_pad)
    cuts = []
    acc = 0
    for s in splits:
        cuts.append((acc, acc + s))
        acc += s
    (a_q, a_f, a_i, a_g, b_u, b_v, b_g, c_q, c_k, c_v, c_g, c_qi, c_ki, c_wi) = [z[..., a:b] for a, b in cuts]
    f = lb + (1.0 - lb) * jax.nn.sigmoid(a_f)
    heads = lambda t, nh: t.reshape(bn, length, nh, -1)
    return {
        'hg_q': heads(a_q, HG_HEADS), 'hg_k': heads(1.0 - f, HG_HEADS), 'hg_v': heads(a_i, HG_HEADS),
        'hg_logf': heads(jnp.log(f), HG_HEADS), 'hg_gate': a_g,
        'gm_u': jax.nn.gelu(b_u), 'gm_v': jax.nn.gelu(b_v), 'gm_gate': b_g,
        'q': heads(c_q, ATT_HEADS), 'k': heads(c_k, ATT_HEADS), 'v': heads(c_v, ATT_HEADS), 'att_gate': c_g,
        'qi': heads(c_qi, IDX_HEADS), 'ki': c_ki, 'wi': c_wi,
    }


def _hgrn2_recurrence(q, k, v, logf, s0):
    bn, length, nh, _ = q.shape
    dv = v.shape[-1]
    c = math.gcd(length, HG_CHUNK)
    nc = length // c

    def chunks(a):
        return a.astype(F32).reshape(bn, nc, c, nh, a.shape[-1]).transpose(1, 0, 3, 2, 4)

    causal = jnp.tril(jnp.ones((c, c), dtype=bool))

    def step(s, inp):
        qc, kc, vc, gc = inp
        b = jnp.cumsum(gc, axis=2)
        o_inter = jnp.einsum('bhtk,bhkv->bhtv', qc * jnp.exp(b), s)
        diff = b[:, :, :, None, :] - b[:, :, None, :, :]
        decay = jnp.exp(jnp.where(causal[:, :, None], diff, -jnp.inf))
        att = jnp.einsum('bhtk,bhtsk,bhsk->bhts', qc, decay, kc)
        o = o_inter + jnp.einsum('bhts,bhsv->bhtv', att, vc)
        b_last = b[:, :, -1:, :]
        s = jnp.exp(b_last[:, :, 0, :])[..., None] * s + jnp.einsum('bhsk,bhsv->bhkv', kc * jnp.exp(b_last - b), vc)
        return s, o

    s_fin, o = lax.scan(step, s0.astype(F32), (chunks(q), chunks(k), chunks(v), chunks(logf)))
    o = o.transpose(1, 0, 3, 2, 4).reshape(bn, length, nh, dv)
    return o.astype(v.dtype), s_fin


def _gmlp_branch(u, v, g_v, w_s, b_s):
    bn, length, width = u.shape
    dg = width // GM_GROUPS
    vn = _rms_norm(v, g_v)
    padded = -(-length // GM_CHUNK) * GM_CHUNK
    vp = jnp.pad(vn.reshape(bn, length, GM_GROUPS, dg), ((0, 0), (0, padded - length), (0, 0), (0, 0)))
    vc = vp.reshape(bn, padded // GM_CHUNK, GM_CHUNK, GM_GROUPS, dg)
    w_causal = w_s * jnp.tril(jnp.ones((GM_CHUNK, GM_CHUNK), w_s.dtype))
    mixed = jnp.einsum('gts,bcsgd->bctgd', w_causal, vc) + b_s.T[None, None, :, :, None]
    mixed = mixed.reshape(bn, padded, GM_GROUPS, dg)[:, :length]
    return u * mixed.reshape(bn, length, width), vn


def _merge(x, p, o_hg, g_hg, gm_out, att_out, w_out):
    bn, length, d = x.shape
    hg_dim = o_hg.shape[-1]
    y_hg = _rms_norm(o_hg, g_hg.reshape(HG_HEADS, hg_dim)).reshape(bn, length, HG_HEADS * hg_dim)
    cat = jnp.concatenate([
        y_hg * jax.nn.silu(p['hg_gate']),
        gm_out * jax.nn.silu(p['gm_gate']),
        att_out.reshape(bn, length, -1) * jax.nn.silu(p['att_gate']),
    ], axis=-1)
    m = bn * length
    return _norm_matmul(cat.reshape(m, -1), w_out, res=x.reshape(m, d)).reshape(bn, length, d)


def _cross_attend(x, g, mk, mv, w_q, w_o):
    bn, length, d = x.shape
    m = bn * length
    xw = w_q.shape[1]
    xd = xw // X_HEADS
    q = _norm_matmul(x.reshape(m, d), w_q, g=g).reshape(bn, length, X_HEADS, xd)
    logits = jnp.einsum('blhd,bmhd->bhlm', q, mk) * (xd ** -0.5)
    pr = jax.nn.softmax(logits, axis=-1)
    o = jnp.einsum('bhlm,bmhd->blhd', pr, mv)
    return _norm_matmul(o.reshape(m, xw), w_o, res=x.reshape(m, d)).reshape(bn, length, d)


def kernel(x_prompt, x_sample, cache_k, cache_v, cache_kidx, cache_mem_k, cache_mem_v, state_hgrn, page_table, mem_prompt,
           g_mix, w_in, hg_lb, g_hg, g_gm, w_spatial, b_spatial, w_out, g_xattn, g_mem, w_xq, w_xk, w_xv, w_xo, g_final):
    depth, d_model, _ = w_in.shape
    mix = w_out.shape[1]
    hgw, gmw, attw = mix // 4, mix // 4, mix // 2
    splits = (hgw,) * 4 + (gmw,) * 3 + (attw,) * 4 + (IDX_HEADS * IDX_DIM, IDX_DIM, IDX_HEADS)
    hg_dim = hgw // HG_HEADS
    p_lb = jax.nn.softmax(hg_lb.astype(F32), axis=0)
    lbs = jnp.cumsum(p_lb, axis=0) - p_lb[0]
    xp, xs = x_prompt, x_sample
    bp, mem_len, _ = mem_prompt.shape
    pk, pv, pki, phg, pmk, pmv = [], [], [], [], [], []
    sk, sv, ski, shg, sgv = [], [], [], [], []
    for l in range(depth):
        pp = _project(xp, g_mix[l], w_in[l], lbs[l], splits)
        s0 = jnp.zeros((xp.shape[0], HG_HEADS, hg_dim, hg_dim), F32)
        o_hg, s_fin = _hgrn2_recurrence(pp['hg_q'], pp['hg_k'], pp['hg_v'], pp['hg_logf'], s0)
        gm_out, _ = _gmlp_branch(pp['gm_u'], pp['gm_v'], g_gm[l], w_spatial[l], b_spatial[l])
        att = _dsa_prompt(pp['q'], pp['k'], pp['v'], pp['qi'], pp['ki'], pp['wi'])
        xp = _merge(xp, pp, o_hg, g_hg[l], gm_out, att, w_out[l])
        mem2 = mem_prompt.reshape(bp * mem_len, d_model)
        xw = w_xk.shape[2]
        mk = _norm_matmul(mem2, w_xk[l], g=g_mem[l]).reshape(bp, mem_len, X_HEADS, xw // X_HEADS)
        mv = _norm_matmul(mem2, w_xv[l], g=g_mem[l]).reshape(bp, mem_len, X_HEADS, xw // X_HEADS)
        xp = _cross_attend(xp, g_xattn[l], mk, mv, w_xq[l], w_xo[l])
        pk.append(pp['k']); pv.append(pp['v']); pki.append(pp['ki']); phg.append(s_fin); pmk.append(mk); pmv.append(mv)

        ps = _project(xs, g_mix[l], w_in[l], lbs[l], splits)
        o_hg_s, s_new = _hgrn2_recurrence(ps['hg_q'], ps['hg_k'], ps['hg_v'], ps['hg_logf'], state_hgrn[l])
        gm_out_s, vn_s = _gmlp_branch(ps['gm_u'], ps['gm_v'], g_gm[l], w_spatial[l], b_spatial[l])
        att_s = _dsa_sample(ps['q'], ps['k'], ps['v'], ps['qi'], ps['ki'], ps['wi'],
                            cache_k, cache_v, cache_kidx, l, page_table)
        xs = _merge(xs, ps, o_hg_s, g_hg[l], gm_out_s, att_s, w_out[l])
        xs = _cross_attend(xs, g_xattn[l], cache_mem_k[l], cache_mem_v[l], w_xq[l], w_xo[l])
        sk.append(ps['k']); sv.append(ps['v']); ski.append(ps['ki']); shg.append(s_new); sgv.append(vn_s)
    y_prompt = _rms_norm(xp, g_final)
    y_sample = _rms_norm(xs, g_final)
    return (y_prompt, y_sample,
            jnp.stack(pk), jnp.stack(pv), jnp.stack(pki), jnp.stack(phg), jnp.stack(pmk), jnp.stack(pmv),
            jnp.stack(sk), jnp.stack(sv), jnp.stack(ski), jnp.stack(shg), jnp.stack(sgv))
```

```python
import functools
import math

import jax
import jax.numpy as jnp
from jax import lax
from jax.experimental import pallas as pl
from jax.experimental.pallas import tpu as pltpu

F32 = jnp.float32
BF16 = jnp.bfloat16

HG_HEADS = 4
HG_CHUNK = 64
GM_GROUPS = 4
GM_CHUNK = 128
ATT_HEADS = 8
ATT_DIM = 64
IDX_HEADS = 8
IDX_DIM = 64
TOPK_MAX = 256
PAGE_SIZE = 128
X_HEADS = 4
RMS_EPS = 1e-6

LANES = 128
VMEM_LIMIT_BYTES = 48 * 1024 * 1024

INT_MIN = -(2 ** 31)
NEG_BIG = -1e30


def _norm_matmul_kernel(*refs, norm, has_res):
    if has_res:
        x_ref, g_ref, w_ref, r_ref, o_ref, h_sc = refs
    else:
        x_ref, g_ref, w_ref, o_ref, h_sc = refs

    @pl.when(pl.program_id(1) == 0)
    def _():
        x = x_ref[...]
        if norm:
            ms = jnp.mean(x * x, axis=-1, keepdims=True)
            x = x * lax.rsqrt(ms + RMS_EPS) * g_ref[...]
        h_sc[...] = x.astype(BF16)

    acc = jnp.dot(h_sc[...], w_ref[...], preferred_element_type=F32)
    if has_res:
        acc = acc + r_ref[...]
    o_ref[...] = acc


def _norm_matmul(x, w, g=None, res=None, tm=512, tn=512):
    m, k = x.shape
    n = w.shape[1]
    tm = min(tm, m)
    tn = min(tn, n)
    assert m % tm == 0 and n % tn == 0
    norm = g is not None
    g2 = (g if norm else jnp.ones((k,), F32)).reshape(1, k).astype(F32)
    in_specs = [
        pl.BlockSpec((tm, k), lambda i, j: (i, 0)),
        pl.BlockSpec((1, k), lambda i, j: (0, 0)),
        pl.BlockSpec((k, tn), lambda i, j: (0, j)),
    ]
    args = [x, g2, w.astype(BF16)]
    if res is not None:
        in_specs.append(pl.BlockSpec((tm, tn), lambda i, j: (i, j)))
        args.append(res)
    return pl.pallas_call(
        functools.partial(_norm_matmul_kernel, norm=norm, has_res=res is not None),
        grid=(m // tm, n // tn),
        in_specs=in_specs,
        out_specs=pl.BlockSpec((tm, tn), lambda i, j: (i, j)),
        out_shape=jax.ShapeDtypeStruct((m, n), F32),
        scratch_shapes=[pltpu.VMEM((tm, k), BF16)],
        compiler_params=pltpu.CompilerParams(
            dimension_semantics=("arbitrary", "arbitrary"), vmem_limit_bytes=VMEM_LIMIT_BYTES),
    )(*args)


def _dsa_prompt_kernel(qT_ref, qiT_ref, wiT_ref, k_ref, vT_ref, ki_ref, o_ref,
                       key_sc, bias_sc, m_sc, l_sc, acc_sc, *, topk):
    tq = qT_ref.shape[4] // 2
    ck = tq
    n_pairs = k_ref.shape[1]
    dh = vT_ref.shape[3]
    i = pl.program_id(1)
    n = i + 1
    row = lax.broadcasted_iota(jnp.int32, (ck, tq), 0)
    col = lax.broadcasted_iota(jnp.int32, (ck, tq), 1)

    def chunk(c):
        return pl.ds(pl.multiple_of(c * ck, ck), ck)

    def score_body(c, carry):
        ki_c = ki_ref[0, chunk(c), :]
        acc = jnp.zeros((ck, tq), F32)
        for h in range(IDX_HEADS):
            s = jnp.dot(ki_c, qiT_ref[0, 0, :, h * tq:(h + 1) * tq], preferred_element_type=F32)
            acc = acc + jnp.maximum(s, 0.0) * wiT_ref[0, 0, h:h + 1, :]
        bits = pltpu.bitcast(acc, jnp.int32)
        key = bits ^ ((bits >> 31) & 0x7FFFFFFF)
        key = jnp.where(key == -1, 0, key)
        adm = (row + c * ck) <= (col + i * tq)
        key_sc[chunk(c), :] = jnp.where(adm, key, INT_MIN)
        return carry

    lax.fori_loop(0, n, score_body, 0)

    def bit_body(bi, carry):
        thr, n_gt = carry
        cand = thr + lax.shift_left(jnp.int32(1), 31 - bi)

        def cnt_body(c, a):
            ge = jnp.where(key_sc[chunk(c), :] >= cand, 1.0, 0.0)
            part = ge.reshape(4, ck // 4, tq).sum(axis=0)
            return a + part.reshape(ck // 32, 8, tq).sum(axis=0)

        a = lax.fori_loop(0, n, cnt_body, jnp.zeros((8, tq), F32))
        total = a.sum(axis=0, keepdims=True)
        ok = total >= topk
        return jnp.where(ok, cand, thr), jnp.where(ok, n_gt, total)

    thr, n_gt = lax.fori_loop(
        0, 32, bit_body, (jnp.full((1, tq), INT_MIN, jnp.int32), jnp.zeros((1, tq), F32)))

    need = topk - n_gt
    tri = jnp.where(row > col, 1.0, 0.0).astype(BF16)

    def bias_body(c, eq_before):
        kk = key_sc[chunk(c), :]
        eq = kk == thr
        eqf = jnp.where(eq, 1.0, 0.0)
        rank = jnp.dot(tri, eqf.astype(BF16), preferred_element_type=F32) + eq_before
        tie = jnp.where(rank < need, jnp.where(kk != INT_MIN, 0.0, NEG_BIG), NEG_BIG)
        bias_sc[chunk(c), :] = jnp.where(kk > thr, 0.0, jnp.where(eq, tie, NEG_BIG))
        return eq_before + eqf.sum(axis=0, keepdims=True)

    lax.fori_loop(0, n, bias_body, jnp.zeros((1, tq), F32))

    m_sc[...] = jnp.full(m_sc.shape, NEG_BIG, F32)
    l_sc[...] = jnp.zeros(l_sc.shape, F32)
    acc_sc[...] = jnp.zeros(acc_sc.shape, F32)

    def att_body(c, carry):
        bias = bias_sc[chunk(c), :]
        for pr in range(n_pairs):
            s2 = jnp.dot(k_ref[0, pr, chunk(c), :], qT_ref[0, 0, pr], preferred_element_type=F32)
            for e in range(2):
                h = 2 * pr + e
                s = s2[:, e * tq:(e + 1) * tq] + bias
                m_old = m_sc[h:h + 1, :]
                m_new = jnp.maximum(m_old, s.max(axis=0, keepdims=True))
                alpha = jnp.exp2(m_old - m_new)
                p = jnp.exp2(s - m_new)
                l_sc[h:h + 1, :] = alpha * l_sc[h:h + 1, :] + p.sum(axis=0, keepdims=True)
                m_sc[h:h + 1, :] = m_new
                pv = jnp.dot(vT_ref[0, h, c], p.astype(BF16), preferred_element_type=F32)
                acc_sc[h] = alpha * acc_sc[h] + pv
        return carry

    lax.fori_loop(0, n, att_body, 0)
    for h in range(2 * n_pairs):
        o_ref[0, 0, h * dh:(h + 1) * dh, :] = acc_sc[h] / l_sc[h:h + 1, :]


def _dsa_prompt(q, k, v, qi, ki, wi, tq=256):
    bn, length, nh, dh = q.shape
    tq = min(tq, length)
    nq = length // tq
    topk = min(TOPK_MAX, length // 4)
    scale = (ATT_DIM ** -0.5) * math.log2(math.e)
    qT = (q * scale).astype(BF16).reshape(bn, nq, tq, nh // 2, 2, dh).transpose(0, 1, 3, 4, 5, 2)
    eye2 = jnp.eye(2, dtype=BF16)
    qT_bd = jnp.einsum('bnpedt,ef->bnpedft', qT, eye2).reshape(bn, nq, nh // 2, 2 * dh, 2 * tq)
    qiT = qi.astype(BF16).reshape(bn, nq, tq, IDX_HEADS, IDX_DIM).transpose(0, 1, 4, 3, 2)
    qiT = qiT.reshape(bn, nq, IDX_DIM, IDX_HEADS * tq)
    wiT = wi.astype(F32).reshape(bn, nq, tq, IDX_HEADS).transpose(0, 1, 3, 2)
    k_pairs = k.astype(BF16).reshape(bn, length, nh // 2, 2 * dh).transpose(0, 2, 1, 3)
    vT = v.astype(BF16).reshape(bn, nq, tq, nh, dh).transpose(0, 3, 1, 4, 2)
    ki_b = ki.astype(BF16)
    oT = pl.pallas_call(
        functools.partial(_dsa_prompt_kernel, topk=float(topk)),
        grid=(bn, nq),
        in_specs=[
            pl.BlockSpec((1, 1, nh // 2, 2 * dh, 2 * tq), lambda b, i: (b, i, 0, 0, 0)),
            pl.BlockSpec((1, 1, IDX_DIM, IDX_HEADS * tq), lambda b, i: (b, i, 0, 0)),
            pl.BlockSpec((1, 1, IDX_HEADS, tq), lambda b, i: (b, i, 0, 0)),
            pl.BlockSpec((1, nh // 2, length, 2 * dh), lambda b, i: (b, 0, 0, 0)),
            pl.BlockSpec((1, nh, nq, dh, tq), lambda b, i: (b, 0, 0, 0, 0)),
            pl.BlockSpec((1, length, IDX_DIM), lambda b, i: (b, 0, 0)),
        ],
        out_specs=pl.BlockSpec((1, 1, nh * dh, tq), lambda b, i: (b, i, 0, 0)),
        out_shape=jax.ShapeDtypeStruct((bn, nq, nh * dh, tq), F32),
        scratch_shapes=[
            pltpu.VMEM((length, tq), jnp.int32), pltpu.VMEM((length, tq), F32),
            pltpu.VMEM((nh, tq), F32), pltpu.VMEM((nh, tq), F32), pltpu.VMEM((nh, dh, tq), F32)],
        compiler_params=pltpu.CompilerParams(
            dimension_semantics=("arbitrary", "arbitrary"), vmem_limit_bytes=VMEM_LIMIT_BYTES),
    )(qT_bd, qiT, wiT, k_pairs, vT, ki_b)
    return oT.transpose(0, 1, 3, 2).reshape(bn, length, nh, dh)


QPAD = 8


def _order_key(score):
    bits = pltpu.bitcast(score, jnp.int32)
    key = bits ^ ((bits >> 31) & 0x7FFFFFFF)
    return jnp.where(key == -1, 0, key)


def _sample_index_kernel(pt_ref, qi_ref, wcol_ref, kin_ref, *rest, pp, n_new, topk):
    page_refs = rest[:pp]
    bias_ref, key_sc = rest[pp], rest[pp + 1]
    g = pl.program_id(1)
    n_pages = pl.num_programs(1) * pp
    qi = qi_ref[0]
    wcol = wcol_ref[0]

    def page_keys(page):
        s = lax.dot_general(qi, page.astype(BF16), (((1,), (1,)), ((), ())), preferred_element_type=F32)
        sc = (jnp.maximum(s, 0.0) * wcol).reshape(IDX_HEADS, QPAD, PAGE_SIZE).sum(axis=0)
        return _order_key(sc)

    for j in range(pp):
        key_sc[g * pp + j] = page_keys(page_refs[j][0, 0])

    @pl.when(g == pl.num_programs(1) - 1)
    def _():
        rowq = lax.broadcasted_iota(jnp.int32, (QPAD, PAGE_SIZE), 0)
        lane = lax.broadcasted_iota(jnp.int32, (QPAD, PAGE_SIZE), 1)
        adm_new = (lane <= rowq) & (lane < n_new)
        key_sc[n_pages] = jnp.where(adm_new, page_keys(kin_ref[0]), INT_MIN)
        n_all = n_pages + 1

        def count_ge(cand):
            cand_b = jnp.broadcast_to(cand, (QPAD, PAGE_SIZE))
            a = lax.fori_loop(
                0, n_all, lambda p, a: a + jnp.where(key_sc[p] >= cand_b, 1.0, 0.0),
                jnp.zeros((QPAD, PAGE_SIZE), F32), unroll=4)
            return a.sum(axis=1, keepdims=True)

        def bit_body(bi, carry):
            thr, n_gt = carry
            cand = thr + lax.shift_left(jnp.int32(1), 31 - bi)
            total = count_ge(cand)
            ok = total >= topk
            return jnp.where(ok, cand, thr), jnp.where(ok, n_gt, total)

        thr, n_gt = lax.fori_loop(
            0, 32, bit_body, (jnp.full((QPAD, 1), INT_MIN, jnp.int32), jnp.zeros((QPAD, 1), F32)))
        n_ge = count_ge(thr)
        thr_b = jnp.broadcast_to(thr, (QPAD, PAGE_SIZE))

        def fast_body(p, carry):
            kk = key_sc[p]
            bias_ref[0, p] = jnp.where(kk >= thr_b, jnp.where(kk != INT_MIN, 0.0, NEG_BIG), NEG_BIG)
            return carry

        lax.fori_loop(0, n_all, fast_body, 0)

        @pl.when(jnp.max(n_ge) > topk)
        def _():
            need = jnp.broadcast_to(topk - n_gt, (QPAD, PAGE_SIZE))
            r = lax.broadcasted_iota(jnp.int32, (PAGE_SIZE, PAGE_SIZE), 0)
            c = lax.broadcasted_iota(jnp.int32, (PAGE_SIZE, PAGE_SIZE), 1)
            before = jnp.where(r < c, 1.0, 0.0).astype(BF16)

            def tie_body(p, eq_before):
                kk = key_sc[p]
                eq = kk == thr_b
                eqf = jnp.where(eq, 1.0, 0.0)
                rank = jnp.dot(eqf.astype(BF16), before, preferred_element_type=F32) + eq_before
                tie = jnp.where(rank < need, jnp.where(kk != INT_MIN, 0.0, NEG_BIG), NEG_BIG)
                bias_ref[0, p] = jnp.where(kk > thr_b, 0.0, jnp.where(eq, tie, NEG_BIG))
                return eq_before + eqf.sum(axis=1, keepdims=True)

            lax.fori_loop(0, n_all, tie_body, jnp.zeros((QPAD, 1), F32))


def _sample_attn_kernel(pt_ref, q_ref, bias_ref, biasn_ref, kn_ref, vn_ref, *rest, pp):
    k_refs, v_refs = rest[:pp], rest[pp:2 * pp]
    o_ref, m_sc, l_sc, acc_sc = rest[2 * pp:]
    g = pl.program_id(1)
    qbd = q_ref[0]

    @pl.when(g == 0)
    def _():
        m_sc[...] = jnp.full(m_sc.shape, NEG_BIG, F32)
        l_sc[...] = jnp.zeros(l_sc.shape, F32)
        acc_sc[...] = jnp.zeros(acc_sc.shape, F32)

    def update(ks, vs, biases):
        width = PAGE_SIZE * len(ks)
        logits = jnp.concatenate(
            [lax.dot_general(qbd, kp.astype(BF16), (((1,), (1,)), ((), ())), preferred_element_type=F32) for kp in ks],
            axis=1)
        bias = jnp.concatenate(biases, axis=1)
        s = (logits.reshape(ATT_HEADS, QPAD, width) + bias[None]).reshape(ATT_HEADS * QPAD, width)
        m_old = m_sc[...]
        m_new = jnp.maximum(m_old, s.max(axis=1, keepdims=True))
        alpha = jnp.exp2(m_old - m_new)
        p = jnp.exp2(s - m_new)
        l_sc[...] = alpha * l_sc[...] + p.sum(axis=1, keepdims=True)
        m_sc[...] = m_new
        pv = jnp.zeros(acc_sc.shape, F32)
        for j, vp in enumerate(vs):
            pv = pv + jnp.dot(p[:, j * PAGE_SIZE:(j + 1) * PAGE_SIZE].astype(BF16), vp.astype(BF16),
                              preferred_element_type=F32)
        acc_sc[...] = alpha * acc_sc[...] + pv

    update([r[0, 0] for r in k_refs], [r[0, 0] for r in v_refs], [bias_ref[0, j] for j in range(pp)])

    @pl.when(g == pl.num_programs(1) - 1)
    def _():
        update([kn_ref[0]], [vn_ref[0]], [biasn_ref[0, 0]])
        o_ref[0] = acc_sc[...] / l_sc[...]


def _dsa_sample(q, k, v, qi, ki, wi, cache_k, cache_v, cache_kidx, layer, page_table, pp_idx=16, pp_att=4):
    bn, n_new, nh, dh = q.shape
    n_pages = page_table.shape[1]
    depth, n_pool = cache_k.shape[:2]
    topk = min(TOPK_MAX, (n_pages * PAGE_SIZE + n_new) // 4)
    pp_idx = math.gcd(pp_idx, n_pages)
    pp_att = math.gcd(pp_att, n_pages)
    pt_flat = page_table.reshape(-1).astype(jnp.int32)
    padq = lambda a: jnp.pad(a, ((0, 0), (0, 0), (0, QPAD - n_new)) + ((0, 0),) * (a.ndim - 3))
    rows_i = IDX_HEADS * QPAD
    qi_rows = padq(qi.astype(BF16).transpose(0, 2, 1, 3)).reshape(bn, rows_i, IDX_DIM)
    wcol = jnp.broadcast_to(padq(wi.astype(F32).transpose(0, 2, 1)).reshape(bn, rows_i, 1), (bn, rows_i, PAGE_SIZE))
    padk = lambda a: jnp.pad(a.astype(F32), ((0, 0), (0, PAGE_SIZE - n_new), (0, 0)))
    ki_new = padk(ki)

    def page_spec_idx(j):
        return pl.BlockSpec((1, 1, PAGE_SIZE, IDX_DIM),
                            lambda b, g, pt, j=j: (layer, pt[b * n_pages + g * pp_idx + j], 0, 0))

    bias = pl.pallas_call(
        functools.partial(_sample_index_kernel, pp=pp_idx, n_new=n_new, topk=float(topk)),
        grid_spec=pltpu.PrefetchScalarGridSpec(
            num_scalar_prefetch=1,
            grid=(bn, n_pages // pp_idx),
            in_specs=[
                pl.BlockSpec((1, rows_i, IDX_DIM), lambda b, g, pt: (b, 0, 0)),
                pl.BlockSpec((1, rows_i, PAGE_SIZE), lambda b, g, pt: (b, 0, 0)),
                pl.BlockSpec((1, PAGE_SIZE, IDX_DIM), lambda b, g, pt: (b, 0, 0)),
            ] + [page_spec_idx(j) for j in range(pp_idx)],
            out_specs=pl.BlockSpec((1, n_pages + 1, QPAD, PAGE_SIZE), lambda b, g, pt: (b, 0, 0, 0)),
            scratch_shapes=[pltpu.VMEM((n_pages + 1, QPAD, PAGE_SIZE), jnp.int32)],
        ),
        out_shape=jax.ShapeDtypeStruct((bn, n_pages + 1, QPAD, PAGE_SIZE), F32),
        compiler_params=pltpu.CompilerParams(
            dimension_semantics=("arbitrary", "arbitrary"), vmem_limit_bytes=VMEM_LIMIT_BYTES),
    )(pt_flat, qi_rows, wcol, ki_new, *([cache_kidx] * pp_idx))

    width = nh * dh
    rows_a = nh * QPAD
    scale = (ATT_DIM ** -0.5) * math.log2(math.e)
    q_rows = padq((q * scale).astype(BF16).transpose(0, 2, 1, 3))
    q_bd = jnp.einsum('bhjd,hg->bhjgd', q_rows, jnp.eye(nh, dtype=BF16)).reshape(bn, rows_a, width)
    k_new = padk(k.reshape(bn, n_new, width))
    v_new = padk(v.reshape(bn, n_new, width))
    ck = cache_k.reshape(depth, n_pool, PAGE_SIZE, width)
    cv = cache_v.reshape(depth, n_pool, PAGE_SIZE, width)

    def page_spec_att(j):
        return pl.BlockSpec((1, 1, PAGE_SIZE, width),
                            lambda b, g, pt, j=j: (layer, pt[b * n_pages + g * pp_att + j], 0, 0))

    o_rows = pl.pallas_call(
        functools.partial(_sample_attn_kernel, pp=pp_att),
        grid_spec=pltpu.PrefetchScalarGridSpec(
            num_scalar_prefetch=1,
            grid=(bn, n_pages // pp_att),
            in_specs=[
                pl.BlockSpec((1, rows_a, width), lambda b, g, pt: (b, 0, 0)),
                pl.BlockSpec((1, pp_att, QPAD, PAGE_SIZE), lambda b, g, pt: (b, g, 0, 0)),
                pl.BlockSpec((1, 1, QPAD, PAGE_SIZE), lambda b, g, pt: (b, n_pages, 0, 0)),
                pl.BlockSpec((1, PAGE_SIZE, width), lambda b, g, pt: (b, 0, 0)),
                pl.BlockSpec((1, PAGE_SIZE, width), lambda b, g, pt: (b, 0, 0)),
            ] + [page_spec_att(j) for j in range(pp_att)] * 2,
            out_specs=pl.BlockSpec((1, rows_a, width), lambda b, g, pt: (b, 0, 0)),
            scratch_shapes=[pltpu.VMEM((rows_a, 1), F32), pltpu.VMEM((rows_a, 1), F32),
                            pltpu.VMEM((rows_a, width), F32)],
        ),
        out_shape=jax.ShapeDtypeStruct((bn, rows_a, width), F32),
        compiler_params=pltpu.CompilerParams(
            dimension_semantics=("arbitrary", "arbitrary"), vmem_limit_bytes=VMEM_LIMIT_BYTES),
    )(pt_flat, q_bd, bias, bias, k_new, v_new, *([ck] * pp_att), *([cv] * pp_att))
    o = o_rows.reshape(bn, nh, QPAD, nh, dh)[:, :, :n_new]
    return jnp.einsum('bhjgd,hg->bjhd', o, jnp.eye(nh, dtype=F32))


HG_SUB = 16
HG_PAD_LOGIT = 30.0


def _hgrn_kernel(aq_ref, af_ref, ai_ref, lb_ref, s0_ref, tri_ref, o_ref, sfin_ref, st_sc, *, n_sub):
    c = pl.program_id(1)
    n_pairs = st_sc.shape[0]
    w = st_sc.shape[1]
    dk = w // 2

    @pl.when(c == 0)
    def _():
        st_sc[...] = jnp.zeros(st_sc.shape, F32)
        for p in range(n_pairs):
            for e in range(2):
                st_sc[p, e * dk:(e + 1) * dk, e * dk:(e + 1) * dk] = s0_ref[0, 2 * p + e]

    lb = lb_ref[...]
    f = lb + (1.0 - lb) * jax.nn.sigmoid(af_ref[0])
    g = jnp.log(f)
    kk = 1.0 - f
    g_hi = g.astype(BF16)
    g_lo = (g - g_hi.astype(F32)).astype(BF16)
    tri = tri_ref[...]
    b = jnp.dot(tri, g_hi, preferred_element_type=F32) + jnp.dot(tri, g_lo, preferred_element_type=F32)
    q = aq_ref[0]
    iv = ai_ref[0]

    r = lax.broadcasted_iota(jnp.int32, (w, w), 0)
    cc = lax.broadcasted_iota(jnp.int32, (w, w), 1)
    same_head = (r // dk) == (cc // dk)
    ones_bd = jnp.where(same_head, 1.0, 0.0).astype(BF16)
    mask_bd = jnp.where(same_head, 1.0, 0.0)
    trow = lax.broadcasted_iota(jnp.int32, (HG_SUB, w), 0)

    for p in range(n_pairs):
        lanes = slice(p * w, (p + 1) * w)
        st = st_sc[p]
        for j in range(n_sub):
            rows = slice(j * HG_SUB, (j + 1) * HG_SUB)
            b_blk = b[rows, lanes]
            brel = b_blk if j == 0 else b_blk - b[j * HG_SUB - 1:j * HG_SUB, lanes]
            q_blk, kk_blk, i_blk = q[rows, lanes], kk[rows, lanes], iv[rows, lanes]
            o_blk = lax.dot_general((q_blk * jnp.exp(brel)).astype(BF16), st.astype(BF16),
                                    (((1,), (1,)), ((), ())), preferred_element_type=F32)
            for s in range(HG_SUB):
                e = jnp.where(trow >= s, brel - brel[s:s + 1, :], NEG_BIG)
                x = q_blk * jnp.exp(e) * kk_blk[s:s + 1, :]
                a_s = jnp.dot(x.astype(BF16), ones_bd, preferred_element_type=F32)
                o_blk = o_blk + a_s * i_blk[s:s + 1, :]
            o_ref[0, rows, lanes] = o_blk
            b_last = brel[HG_SUB - 1:HG_SUB, :]
            kt = kk_blk * jnp.exp(b_last - brel)
            upd = lax.dot_general(i_blk.astype(BF16), kt.astype(BF16), (((0,), (0,)), ((), ())),
                                  preferred_element_type=F32)
            st = jnp.exp(b_last) * st + mask_bd * upd
        st_sc[p] = st

    @pl.when(c == pl.num_programs(1) - 1)
    def _():
        for p in range(n_pairs):
            for e in range(2):
                sfin_ref[0, 2 * p + e] = st_sc[p, e * dk:(e + 1) * dk, e * dk:(e + 1) * dk]


def _hgrn2(z, col0, lb, s0, bn, length, width, t_blk=128):
    dk = width // HG_HEADS
    lpad = -(-length // HG_SUB) * HG_SUB
    t_blk = min(t_blk, lpad)
    assert lpad % t_blk == 0 and col0 % width == 0
    z3 = z.reshape(bn, length, z.shape[1])
    cb = col0 // width
    if lpad != length:
        z3 = jnp.pad(z3[..., col0:col0 + 3 * width], ((0, 0), (0, lpad - length), (0, 0)),
                     constant_values=HG_PAD_LOGIT)
        cb = 0
    tri = jnp.tril(jnp.ones((t_blk, t_blk), F32)).astype(BF16)
    s0t = jnp.swapaxes(s0.astype(F32), -1, -2)
    col = lambda k: pl.BlockSpec((1, t_blk, width), lambda b, c, k=k: (b, c, cb + k))
    o, sfin_t = pl.pallas_call(
        functools.partial(_hgrn_kernel, n_sub=t_blk // HG_SUB),
        grid=(bn, lpad // t_blk),
        in_specs=[
            col(0), col(1), col(2),
            pl.BlockSpec((1, width), lambda b, c: (0, 0)),
            pl.BlockSpec((1, HG_HEADS, dk, dk), lambda b, c: (b, 0, 0, 0)),
            pl.BlockSpec((t_blk, t_blk), lambda b, c: (0, 0)),
        ],
        out_specs=[
            pl.BlockSpec((1, t_blk, width), lambda b, c: (b, c, 0)),
            pl.BlockSpec((1, HG_HEADS, dk, dk), lambda b, c: (b, 0, 0, 0)),
        ],
        out_shape=[jax.ShapeDtypeStruct((bn, lpad, width), F32),
                   jax.ShapeDtypeStruct((bn, HG_HEADS, dk, dk), F32)],
        scratch_shapes=[pltpu.VMEM((HG_HEADS // 2, 2 * dk, 2 * dk), F32)],
        compiler_params=pltpu.CompilerParams(
            dimension_semantics=("arbitrary", "arbitrary"), vmem_limit_bytes=VMEM_LIMIT_BYTES),
    )(z3, z3, z3, lb.reshape(1, width).astype(F32), s0t, tri)
    return o[:, :length], jnp.swapaxes(sfin_t, -1, -2)


def _gmlp_kernel(bu_ref, bv_ref, g_ref, w_ref, bias_ref, out_ref, vn_ref):
    u = jax.nn.gelu(bu_ref[...])
    v = jax.nn.gelu(bv_ref[...])
    vn = v * lax.rsqrt(jnp.mean(v * v, axis=-1, keepdims=True) + RMS_EPS) * g_ref[...]
    vn_ref[...] = vn
    vb = vn.astype(BF16)
    width = vn.shape[1]
    dg = width // GM_GROUPS
    group = lax.broadcasted_iota(jnp.int32, vn.shape, 1) // dg
    mixed = bias_ref[...]
    for gi in range(GM_GROUPS):
        mixed = mixed + jnp.where(group == gi, jnp.dot(w_ref[gi], vb, preferred_element_type=F32), 0.0)
    out_ref[...] = u * mixed


def _gmlp(z, col_u, g_v, w_s, b_s, bn, length, width):
    m = bn * length
    w_causal = w_s * jnp.tril(jnp.ones((GM_CHUNK, GM_CHUNK), w_s.dtype))
    dg = width // GM_GROUPS
    if length % GM_CHUNK == 0:
        w_blk = w_causal
        bias = jnp.repeat(b_s.T, dg, axis=1)
    else:
        assert GM_CHUNK % length == 0 and m % GM_CHUNK == 0
        reps = GM_CHUNK // length
        w_blk = jnp.einsum('ab,gts->gatbs', jnp.eye(reps, dtype=w_s.dtype), w_causal[:, :length, :length])
        w_blk = w_blk.reshape(GM_GROUPS, GM_CHUNK, GM_CHUNK)
        bias = jnp.tile(jnp.repeat(b_s.T[:length], dg, axis=1), (reps, 1))
    assert col_u % width == 0
    cb = col_u // width
    return pl.pallas_call(
        _gmlp_kernel,
        grid=(m // GM_CHUNK,),
        in_specs=[
            pl.BlockSpec((GM_CHUNK, width), lambda i: (i, cb)),
            pl.BlockSpec((GM_CHUNK, width), lambda i: (i, cb + 1)),
            pl.BlockSpec((1, width), lambda i: (0, 0)),
            pl.BlockSpec((GM_GROUPS, GM_CHUNK, GM_CHUNK), lambda i: (0, 0, 0)),
            pl.BlockSpec((GM_CHUNK, width), lambda i: (0, 0)),
        ],
        out_specs=[pl.BlockSpec((GM_CHUNK, width), lambda i: (i, 0))] * 2,
        out_shape=[jax.ShapeDtypeStruct((m, width), F32)] * 2,
        compiler_params=pltpu.CompilerParams(
            dimension_semantics=("arbitrary",), vmem_limit_bytes=VMEM_LIMIT_BYTES),
    )(z, z, g_v.reshape(1, width).astype(F32), w_blk.astype(BF16), bias.astype(F32))


def _xattn_kernel(q_ref, mk_ref, mv_ref, o_ref, *, n_heads):
    dh = q_ref.shape[-1] // n_heads
    scale = dh ** -0.5
    for h in range(n_heads):
        cols = slice(h * dh, (h + 1) * dh)
        qh = (q_ref[0, :, cols] * scale).astype(BF16)
        s = lax.dot_general(qh, mk_ref[0, :, cols].astype(BF16), (((1,), (1,)), ((), ())),
                            preferred_element_type=F32)
        p = jnp.exp(s - s.max(axis=-1, keepdims=True))
        o = jnp.dot(p.astype(BF16), mv_ref[0, :, cols].astype(BF16), preferred_element_type=F32)
        o_ref[0, :, cols] = o / p.sum(axis=-1, keepdims=True)


def _xattn(q, mk, mv, tm=512):
    bn, length, width = q.shape
    mem = mk.shape[1]
    tm = min(tm, length)
    assert length % tm == 0
    return pl.pallas_call(
        functools.partial(_xattn_kernel, n_heads=X_HEADS),
        grid=(bn, length // tm),
        in_specs=[
            pl.BlockSpec((1, tm, width), lambda b, i: (b, i, 0)),
            pl.BlockSpec((1, mem, width), lambda b, i: (b, 0, 0)),
            pl.BlockSpec((1, mem, width), lambda b, i: (b, 0, 0)),
        ],
        out_specs=pl.BlockSpec((1, tm, width), lambda b, i: (b, i, 0)),
        out_shape=jax.ShapeDtypeStruct((bn, length, width), F32),
        compiler_params=pltpu.CompilerParams(
            dimension_semantics=("arbitrary", "arbitrary"), vmem_limit_bytes=VMEM_LIMIT_BYTES),
    )(q, mk.reshape(bn, mem, width), mv.reshape(bn, mem, width))


def _merge_matmul_kernel(ohg_ref, gm_ref, att_ref, g1_ref, g2_ref, g3a_ref, g3b_ref, ghg_ref, w_ref, r_ref,
                         o_ref, h_sc, *, hg_dim):
    @pl.when(pl.program_id(1) == 0)
    def _():
        o = ohg_ref[...]
        wh = o.shape[1]
        r = lax.broadcasted_iota(jnp.int32, (wh, wh), 0) // hg_dim
        c = lax.broadcasted_iota(jnp.int32, (wh, wh), 1) // hg_dim
        ones_bd = jnp.where(r == c, 1.0, 0.0).astype(BF16)
        sq = o * o
        hi = sq.astype(BF16)
        lo = (sq - hi.astype(F32)).astype(BF16)
        ms = (jnp.dot(hi, ones_bd, preferred_element_type=F32)
              + jnp.dot(lo, ones_bd, preferred_element_type=F32)) * (1.0 / hg_dim)
        y = o * lax.rsqrt(ms + RMS_EPS) * ghg_ref[...]
        wg = gm_ref.shape[1]
        wa = g3a_ref.shape[1]
        h_sc[:, 0:wh] = (y * jax.nn.silu(g1_ref[...])).astype(BF16)
        h_sc[:, wh:wh + wg] = (gm_ref[...] * jax.nn.silu(g2_ref[...])).astype(BF16)
        h_sc[:, wh + wg:wh + wg + wa] = (att_ref[:, 0:wa] * jax.nn.silu(g3a_ref[...])).astype(BF16)
        h_sc[:, wh + wg + wa:] = (att_ref[:, wa:] * jax.nn.silu(g3b_ref[...])).astype(BF16)

    o_ref[...] = jnp.dot(h_sc[...], w_ref[...], preferred_element_type=F32) + r_ref[...]


def _merge_matmul(x, z, o_hg, gm_out, att, g_hg, w_out, cols, tm=512, tn=512):
    m, d = x.shape
    wh, wg, wa = o_hg.shape[1], gm_out.shape[1], att.shape[1]
    assert wh == wg and wa == 2 * wh and all(c % wh == 0 for c in cols)
    k = wh + wg + wa
    tm = min(tm, m)
    tn = min(tn, d)
    c1, c2, c3 = (c // wh for c in cols)
    row = lambda width: pl.BlockSpec((tm, width), lambda i, j: (i, 0))
    zcol = lambda cidx: pl.BlockSpec((tm, wh), lambda i, j, cidx=cidx: (i, cidx))
    return pl.pallas_call(
        functools.partial(_merge_matmul_kernel, hg_dim=wh // HG_HEADS),
        grid=(m // tm, d // tn),
        in_specs=[row(wh), row(wg), row(wa), zcol(c1), zcol(c2), zcol(c3), zcol(c3 + 1),
                  pl.BlockSpec((1, wh), lambda i, j: (0, 0)),
                  pl.BlockSpec((k, tn), lambda i, j: (0, j)),
                  pl.BlockSpec((tm, tn), lambda i, j: (i, j))],
        out_specs=pl.BlockSpec((tm, tn), lambda i, j: (i, j)),
        out_shape=jax.ShapeDtypeStruct((m, d), F32),
        scratch_shapes=[pltpu.VMEM((tm, k), BF16)],
        compiler_params=pltpu.CompilerParams(
            dimension_semantics=("arbitrary", "arbitrary"), vmem_limit_bytes=VMEM_LIMIT_BYTES),
    )(o_hg, gm_out, att, z, z, z, z, g_hg.reshape(1, wh).astype(F32), w_out.astype(BF16), x)


def _rmsnorm_kernel(x_ref, g_ref, o_ref):
    x = x_ref[...]
    o_ref[...] = x * lax.rsqrt(jnp.mean(x * x, axis=-1, keepdims=True) + RMS_EPS) * g_ref[...]


def _rmsnorm(x, g, tm=512):
    m, d = x.shape
    tm = min(tm, m)
    return pl.pallas_call(
        _rmsnorm_kernel,
        grid=(m // tm,),
        in_specs=[pl.BlockSpec((tm, d), lambda i: (i, 0)), pl.BlockSpec((1, d), lambda i: (0, 0))],
        out_specs=pl.BlockSpec((tm, d), lambda i: (i, 0)),
        out_shape=jax.ShapeDtypeStruct((m, d), F32),
        compiler_params=pltpu.CompilerParams(dimension_semantics=("arbitrary",), vmem_limit_bytes=VMEM_LIMIT_BYTES),
    )(x, g.reshape(1, d).astype(F32))


def _mix_layer(x, layer, lb, s0, params, dsa_fn):
    g_mix, w_in, g_hg, g_gm, w_spatial, b_spatial, w_out = params
    bn, length, d = x.shape
    m = bn * length
    mix = w_out.shape[0]
    hgw, gmw, attw = mix // 4, mix // 4, mix // 2
    n_in = w_in.shape[1]
    n_pad = -(-n_in // 512) * 512
    x2 = x.reshape(m, d)
    z = _norm_matmul(x2, jnp.pad(w_in, ((0, 0), (0, n_pad - n_in))), g=g_mix)
    c_hg, c_gm, c_att = 0, 4 * hgw, 4 * hgw + 3 * gmw
    c_idx = c_att + 4 * attw
    o_hg, s_fin = _hgrn2(z, c_hg, lb, s0, bn, length, hgw)
    gm_out, vn = _gmlp(z, c_gm, g_gm, w_spatial, b_spatial, bn, length, gmw)
    sec = lambda c, wdt: z[:, c:c + wdt].reshape(bn, length, wdt)
    q = sec(c_att, attw).reshape(bn, length, ATT_HEADS, ATT_DIM)
    k = sec(c_att + attw, attw).reshape(bn, length, ATT_HEADS, ATT_DIM)
    v = sec(c_att + 2 * attw, attw).reshape(bn, length, ATT_HEADS, ATT_DIM)
    qi = sec(c_idx, IDX_HEADS * IDX_DIM).reshape(bn, length, IDX_HEADS, IDX_DIM)
    ki = sec(c_idx + IDX_HEADS * IDX_DIM, IDX_DIM)
    wi = sec(c_idx + IDX_HEADS * IDX_DIM + IDX_DIM, IDX_HEADS)
    att = dsa_fn(q, k, v, qi, ki, wi)
    y = _merge_matmul(x2, z, o_hg.reshape(m, hgw), gm_out, att.reshape(m, attw), g_hg, w_out,
                      (c_hg + 3 * hgw, c_gm + 2 * gmw, c_att + 3 * attw))
    return y.reshape(bn, length, d), k, v, ki, s_fin, vn.reshape(bn, length, gmw)


def _cross_layer(x, g, mk, mv, w_q, w_o):
    bn, length, d = x.shape
    m = bn * length
    x2 = x.reshape(m, d)
    q = _norm_matmul(x2, w_q, g=g).reshape(bn, length, -1)
    o = _xattn(q, mk, mv)
    return _norm_matmul(o.reshape(m, -1), w_o, res=x2).reshape(bn, length, d)


def kernel(x_prompt, x_sample, cache_k, cache_v, cache_kidx, cache_mem_k, cache_mem_v, state_hgrn, page_table, mem_prompt,
           g_mix, w_in, hg_lb, g_hg, g_gm, w_spatial, b_spatial, w_out, g_xattn, g_mem, w_xq, w_xk, w_xv, w_xo, g_final):
    depth, d_model, _ = w_in.shape
    hg_dim = w_out.shape[1] // 4 // HG_HEADS
    p_lb = jax.nn.softmax(hg_lb.astype(F32), axis=0)
    lbs = jnp.cumsum(p_lb, axis=0) - p_lb[0]
    xp, xs = x_prompt, x_sample
    bp, mem_len, _ = mem_prompt.shape
    xw = w_xk.shape[2]
    mem2 = mem_prompt.reshape(bp * mem_len, d_model)
    pk, pv, pki, phg, pmk, pmv = [], [], [], [], [], []
    sk, sv, ski, shg, sgv = [], [], [], [], []
    for l in range(depth):
        params = (g_mix[l], w_in[l], g_hg[l], g_gm[l], w_spatial[l], b_spatial[l], w_out[l])
        s0 = jnp.zeros((xp.shape[0], HG_HEADS, hg_dim, hg_dim), F32)
        xp, k_p, v_p, ki_p, s_fin, _ = _mix_layer(xp, l, lbs[l], s0, params, _dsa_prompt)
        mk = _norm_matmul(mem2, w_xk[l], g=g_mem[l]).reshape(bp, mem_len, X_HEADS, xw // X_HEADS)
        mv = _norm_matmul(mem2, w_xv[l], g=g_mem[l]).reshape(bp, mem_len, X_HEADS, xw // X_HEADS)
        xp = _cross_layer(xp, g_xattn[l], mk, mv, w_xq[l], w_xo[l])
        pk.append(k_p); pv.append(v_p); pki.append(ki_p); phg.append(s_fin); pmk.append(mk); pmv.append(mv)

        dsa_s = functools.partial(_dsa_sample, cache_k=cache_k, cache_v=cache_v, cache_kidx=cache_kidx,
                                  layer=l, page_table=page_table)
        xs, k_s, v_s, ki_s, s_new, vn_s = _mix_layer(xs, l, lbs[l], state_hgrn[l], params, dsa_s)
        xs = _cross_layer(xs, g_xattn[l], cache_mem_k[l], cache_mem_v[l], w_xq[l], w_xo[l])
        sk.append(k_s); sv.append(v_s); ski.append(ki_s); shg.append(s_new); sgv.append(vn_s)
    y_prompt = _rmsnorm(xp.reshape(-1, d_model), g_final).reshape(xp.shape)
    y_sample = _rmsnorm(xs.reshape(-1, d_model), g_final).reshape(xs.shape)
    return (y_prompt, y_sample,
            jnp.stack(pk), jnp.stack(pv), jnp.stack(pki), jnp.stack(phg), jnp.stack(pmk), jnp.stack(pmv),
            jnp.stack(sk), jnp.stack(sv), jnp.stack(ski), jnp.stack(shg), jnp.stack(sgv))
```

```python
import functools
import math

import jax
import jax.numpy as jnp
from jax import lax
from jax.experimental import pallas as pl
from jax.experimental.pallas import tpu as pltpu

F32 = jnp.float32
BF16 = jnp.bfloat16

HG_HEADS = 4
HG_CHUNK = 64
GM_GROUPS = 4
GM_CHUNK = 128
ATT_HEADS = 8
ATT_DIM = 64
IDX_HEADS = 8
IDX_DIM = 64
TOPK_MAX = 256
PAGE_SIZE = 128
X_HEADS = 4
RMS_EPS = 1e-6

LANES = 128
VMEM_LIMIT_BYTES = 48 * 1024 * 1024

INT_MIN = -(2 ** 31)
NEG_BIG = -1e30


def _norm_matmul_kernel(*refs, norm, has_res):
    if has_res:
        x_ref, g_ref, w_ref, r_ref, o_ref, h_sc = refs
    else:
        x_ref, g_ref, w_ref, o_ref, h_sc = refs

    @pl.when(pl.program_id(1) == 0)
    def _():
        x = x_ref[...]
        if norm:
            ms = jnp.mean(x * x, axis=-1, keepdims=True)
            x = x * lax.rsqrt(ms + RMS_EPS) * g_ref[...]
        h_sc[...] = x.astype(BF16)

    acc = jnp.dot(h_sc[...], w_ref[...], preferred_element_type=F32)
    if has_res:
        acc = acc + r_ref[...]
    o_ref[...] = acc


def _norm_matmul(x, w, g=None, res=None, tm=512, tn=512):
    m, k = x.shape
    n = w.shape[1]
    tm = min(tm, m)
    tn = min(tn, n)
    assert m % tm == 0 and n % tn == 0
    norm = g is not None
    g2 = (g if norm else jnp.ones((k,), F32)).reshape(1, k).astype(F32)
    in_specs = [
        pl.BlockSpec((tm, k), lambda i, j: (i, 0)),
        pl.BlockSpec((1, k), lambda i, j: (0, 0)),
        pl.BlockSpec((k, tn), lambda i, j: (0, j)),
    ]
    args = [x, g2, w.astype(BF16)]
    if res is not None:
        in_specs.append(pl.BlockSpec((tm, tn), lambda i, j: (i, j)))
        args.append(res)
    return pl.pallas_call(
        functools.partial(_norm_matmul_kernel, norm=norm, has_res=res is not None),
        grid=(m // tm, n // tn),
        in_specs=in_specs,
        out_specs=pl.BlockSpec((tm, tn), lambda i, j: (i, j)),
        out_shape=jax.ShapeDtypeStruct((m, n), F32),
        scratch_shapes=[pltpu.VMEM((tm, k), BF16)],
        compiler_params=pltpu.CompilerParams(
            dimension_semantics=("arbitrary", "arbitrary"), vmem_limit_bytes=VMEM_LIMIT_BYTES),
    )(*args)


def _dsa_prompt_kernel(qT_ref, qiT_ref, wiT_ref, k_ref, vT_ref, ki_ref, o_ref,
                       key_sc, bias_sc, m_sc, l_sc, acc_sc, *, topk):
    tq = qT_ref.shape[4] // 2
    ck = tq
    n_pairs = k_ref.shape[1]
    dh = vT_ref.shape[3]
    i = pl.program_id(1)
    n = i + 1
    row = lax.broadcasted_iota(jnp.int32, (ck, tq), 0)
    col = lax.broadcasted_iota(jnp.int32, (ck, tq), 1)

    def chunk(c):
        return pl.ds(pl.multiple_of(c * ck, ck), ck)

    def score_body(c, carry):
        ki_c = ki_ref[0, chunk(c), :]
        acc = jnp.zeros((ck, tq), F32)
        for h in range(IDX_HEADS):
            s = jnp.dot(ki_c, qiT_ref[0, 0, :, h * tq:(h + 1) * tq], preferred_element_type=F32)
            acc = acc + jnp.maximum(s, 0.0) * wiT_ref[0, 0, h:h + 1, :]
        bits = pltpu.bitcast(acc, jnp.int32)
        key = bits ^ ((bits >> 31) & 0x7FFFFFFF)
        key = jnp.where(key == -1, 0, key)
        adm = (row + c * ck) <= (col + i * tq)
        key_sc[chunk(c), :] = jnp.where(adm, key, INT_MIN)
        return carry

    lax.fori_loop(0, n, score_body, 0)

    def bit_body(bi, carry):
        thr, n_gt = carry
        cand = thr + lax.shift_left(jnp.int32(1), 31 - bi)

        def cnt_body(c, a):
            ge = jnp.where(key_sc[chunk(c), :] >= cand, 1.0, 0.0)
            part = ge.reshape(4, ck // 4, tq).sum(axis=0)
            return a + part.reshape(ck // 32, 8, tq).sum(axis=0)

        a = lax.fori_loop(0, n, cnt_body, jnp.zeros((8, tq), F32))
        total = a.sum(axis=0, keepdims=True)
        ok = total >= topk
        return jnp.where(ok, cand, thr), jnp.where(ok, n_gt, total)

    thr, n_gt = lax.fori_loop(
        0, 32, bit_body, (jnp.full((1, tq), INT_MIN, jnp.int32), jnp.zeros((1, tq), F32)))

    need = topk - n_gt
    tri = jnp.where(row > col, 1.0, 0.0).astype(BF16)

    def bias_body(c, eq_before):
        kk = key_sc[chunk(c), :]
        eq = kk == thr
        eqf = jnp.where(eq, 1.0, 0.0)
        rank = jnp.dot(tri, eqf.astype(BF16), preferred_element_type=F32) + eq_before
        tie = jnp.where(rank < need, jnp.where(kk != INT_MIN, 0.0, NEG_BIG), NEG_BIG)
        bias_sc[chunk(c), :] = jnp.where(kk > thr, 0.0, jnp.where(eq, tie, NEG_BIG))
        return eq_before + eqf.sum(axis=0, keepdims=True)

    lax.fori_loop(0, n, bias_body, jnp.zeros((1, tq), F32))

    m_sc[...] = jnp.full(m_sc.shape, NEG_BIG, F32)
    l_sc[...] = jnp.zeros(l_sc.shape, F32)
    acc_sc[...] = jnp.zeros(acc_sc.shape, F32)

    def att_body(c, carry):
        bias = bias_sc[chunk(c), :]
        for pr in range(n_pairs):
            s2 = jnp.dot(k_ref[0, pr, chunk(c), :], qT_ref[0, 0, pr], preferred_element_type=F32)
            for e in range(2):
                h = 2 * pr + e
                s = s2[:, e * tq:(e + 1) * tq] + bias
                m_old = m_sc[h:h + 1, :]
                m_new = jnp.maximum(m_old, s.max(axis=0, keepdims=True))
                alpha = jnp.exp2(m_old - m_new)
                p = jnp.exp2(s - m_new)
                l_sc[h:h + 1, :] = alpha * l_sc[h:h + 1, :] + p.sum(axis=0, keepdims=True)
                m_sc[h:h + 1, :] = m_new
                pv = jnp.dot(vT_ref[0, h, c], p.astype(BF16), preferred_element_type=F32)
                acc_sc[h] = alpha * acc_sc[h] + pv
        return carry

    lax.fori_loop(0, n, att_body, 0)
    for h in range(2 * n_pairs):
        o_ref[0, 0, h * dh:(h + 1) * dh, :] = acc_sc[h] / l_sc[h:h + 1, :]


def _dsa_prompt(q, k, v, qi, ki, wi, tq=256):
    bn, length, nh, dh = q.shape
    tq = min(tq, length)
    nq = length // tq
    topk = min(TOPK_MAX, length // 4)
    scale = (ATT_DIM ** -0.5) * math.log2(math.e)
    qT = (q * scale).astype(BF16).reshape(bn, nq, tq, nh // 2, 2, dh).transpose(0, 1, 3, 4, 5, 2)
    eye2 = jnp.eye(2, dtype=BF16)
    qT_bd = jnp.einsum('bnpedt,ef->bnpedft', qT, eye2).reshape(bn, nq, nh // 2, 2 * dh, 2 * tq)
    qiT = qi.astype(BF16).reshape(bn, nq, tq, IDX_HEADS, IDX_DIM).transpose(0, 1, 4, 3, 2)
    qiT = qiT.reshape(bn, nq, IDX_DIM, IDX_HEADS * tq)
    wiT = wi.astype(F32).reshape(bn, nq, tq, IDX_HEADS).transpose(0, 1, 3, 2)
    k_pairs = k.astype(BF16).reshape(bn, length, nh // 2, 2 * dh).transpose(0, 2, 1, 3)
    vT = v.astype(BF16).reshape(bn, nq, tq, nh, dh).transpose(0, 3, 1, 4, 2)
    ki_b = ki.astype(BF16)
    oT = pl.pallas_call(
        functools.partial(_dsa_prompt_kernel, topk=float(topk)),
        grid=(bn, nq),
        in_specs=[
            pl.BlockSpec((1, 1, nh // 2, 2 * dh, 2 * tq), lambda b, i: (b, i, 0, 0, 0)),
            pl.BlockSpec((1, 1, IDX_DIM, IDX_HEADS * tq), lambda b, i: (b, i, 0, 0)),
            pl.BlockSpec((1, 1, IDX_HEADS, tq), lambda b, i: (b, i, 0, 0)),
            pl.BlockSpec((1, nh // 2, length, 2 * dh), lambda b, i: (b, 0, 0, 0)),
            pl.BlockSpec((1, nh, nq, dh, tq), lambda b, i: (b, 0, 0, 0, 0)),
            pl.BlockSpec((1, length, IDX_DIM), lambda b, i: (b, 0, 0)),
        ],
        out_specs=pl.BlockSpec((1, 1, nh * dh, tq), lambda b, i: (b, i, 0, 0)),
        out_shape=jax.ShapeDtypeStruct((bn, nq, nh * dh, tq), F32),
        scratch_shapes=[
            pltpu.VMEM((length, tq), jnp.int32), pltpu.VMEM((length, tq), F32),
            pltpu.VMEM((nh, tq), F32), pltpu.VMEM((nh, tq), F32), pltpu.VMEM((nh, dh, tq), F32)],
        compiler_params=pltpu.CompilerParams(
            dimension_semantics=("arbitrary", "arbitrary"), vmem_limit_bytes=VMEM_LIMIT_BYTES),
    )(qT_bd, qiT, wiT, k_pairs, vT, ki_b)
    return oT.transpose(0, 1, 3, 2).reshape(bn, length, nh, dh)


QPAD = 8


def _order_key(score):
    bits = pltpu.bitcast(score, jnp.int32)
    key = bits ^ ((bits >> 31) & 0x7FFFFFFF)
    return jnp.where(key == -1, 0, key)


def _sample_index_kernel(pt_ref, qi_ref, wcol_ref, kin_ref, *rest, pp, n_new, topk):
    page_refs = rest[:pp]
    bias_ref, key_sc = rest[pp], rest[pp + 1]
    g = pl.program_id(1)
    n_pages = pl.num_programs(1) * pp
    qi = qi_ref[0]
    wcol = wcol_ref[0]

    def page_keys(page):
        s = lax.dot_general(qi, page.astype(BF16), (((1,), (1,)), ((), ())), preferred_element_type=F32)
        sc = (jnp.maximum(s, 0.0) * wcol).reshape(IDX_HEADS, QPAD, PAGE_SIZE).sum(axis=0)
        return _order_key(sc)

    for j in range(pp):
        key_sc[g * pp + j] = page_keys(page_refs[j][0, 0])

    @pl.when(g == pl.num_programs(1) - 1)
    def _():
        rowq = lax.broadcasted_iota(jnp.int32, (QPAD, PAGE_SIZE), 0)
        lane = lax.broadcasted_iota(jnp.int32, (QPAD, PAGE_SIZE), 1)
        adm_new = (lane <= rowq) & (lane < n_new)
        key_sc[n_pages] = jnp.where(adm_new, page_keys(kin_ref[0]), INT_MIN)
        n_all = n_pages + 1

        def count_ge(cand):
            cand_b = jnp.broadcast_to(cand, (QPAD, PAGE_SIZE))
            a = lax.fori_loop(
                0, n_all, lambda p, a: a + jnp.where(key_sc[p] >= cand_b, 1.0, 0.0),
                jnp.zeros((QPAD, PAGE_SIZE), F32), unroll=4)
            return a.sum(axis=1, keepdims=True)

        def bit_body(bi, carry):
            thr, n_gt = carry
            cand = thr + lax.shift_left(jnp.int32(1), 31 - bi)
            total = count_ge(cand)
            ok = total >= topk
            return jnp.where(ok, cand, thr), jnp.where(ok, n_gt, total)

        thr, n_gt = lax.fori_loop(
            0, 32, bit_body, (jnp.full((QPAD, 1), INT_MIN, jnp.int32), jnp.zeros((QPAD, 1), F32)))
        n_ge = count_ge(thr)
        thr_b = jnp.broadcast_to(thr, (QPAD, PAGE_SIZE))

        def fast_body(p, carry):
            kk = key_sc[p]
            bias_ref[0, p] = jnp.where(kk >= thr_b, jnp.where(kk != INT_MIN, 0.0, NEG_BIG), NEG_BIG)
            return carry

        lax.fori_loop(0, n_all, fast_body, 0)

        @pl.when(jnp.max(n_ge) > topk)
        def _():
            need = jnp.broadcast_to(topk - n_gt, (QPAD, PAGE_SIZE))
            r = lax.broadcasted_iota(jnp.int32, (PAGE_SIZE, PAGE_SIZE), 0)
            c = lax.broadcasted_iota(jnp.int32, (PAGE_SIZE, PAGE_SIZE), 1)
            before = jnp.where(r < c, 1.0, 0.0).astype(BF16)

            def tie_body(p, eq_before):
                kk = key_sc[p]
                eq = kk == thr_b
                eqf = jnp.where(eq, 1.0, 0.0)
                rank = jnp.dot(eqf.astype(BF16), before, preferred_element_type=F32) + eq_before
                tie = jnp.where(rank < need, jnp.where(kk != INT_MIN, 0.0, NEG_BIG), NEG_BIG)
                bias_ref[0, p] = jnp.where(kk > thr_b, 0.0, jnp.where(eq, tie, NEG_BIG))
                return eq_before + eqf.sum(axis=1, keepdims=True)

            lax.fori_loop(0, n_all, tie_body, jnp.zeros((QPAD, 1), F32))


def _sample_attn_kernel(pt_ref, q_ref, bias_ref, biasn_ref, kn_ref, vn_ref, *rest, pp):
    k_refs, v_refs = rest[:pp], rest[pp:2 * pp]
    o_ref, m_sc, l_sc, acc_sc = rest[2 * pp:]
    g = pl.program_id(1)
    n_heads = q_ref.shape[1] // QPAD

    @pl.when(g == 0)
    def _():
        m_sc[...] = jnp.full(m_sc.shape, NEG_BIG, F32)
        l_sc[...] = jnp.zeros(l_sc.shape, F32)
        acc_sc[...] = jnp.zeros(acc_sc.shape, F32)

    def update(k_pages, v_pages, biases):
        bias = jnp.concatenate(biases, axis=1)
        rows = []
        for h in range(n_heads):
            qh = q_ref[0, h * QPAD:(h + 1) * QPAD, :]
            lg = [lax.dot_general(qh, kp(h).astype(BF16), (((1,), (1,)), ((), ())), preferred_element_type=F32)
                  for kp in k_pages]
            rows.append(jnp.concatenate(lg, axis=1) + bias)
        s = jnp.concatenate(rows, axis=0)
        m_old = m_sc[...]
        m_new = jnp.maximum(m_old, s.max(axis=1, keepdims=True))
        alpha = jnp.exp2(m_old - m_new)
        p = jnp.exp2(s - m_new)
        l_sc[...] = alpha * l_sc[...] + p.sum(axis=1, keepdims=True)
        m_sc[...] = m_new
        pvs = []
        for h in range(n_heads):
            ph = p[h * QPAD:(h + 1) * QPAD].astype(BF16)
            pv = None
            for j, vp in enumerate(v_pages):
                d = jnp.dot(ph[:, j * PAGE_SIZE:(j + 1) * PAGE_SIZE], vp(h).astype(BF16), preferred_element_type=F32)
                pv = d if pv is None else pv + d
            pvs.append(pv)
        acc_sc[...] = alpha * acc_sc[...] + jnp.concatenate(pvs, axis=0)

    def heads_major(r):
        t = pltpu.einshape("tkd->ktd", r[0, 0])
        return lambda h: t[h]

    update([heads_major(r) for r in k_refs], [heads_major(r) for r in v_refs], [bias_ref[0, j] for j in range(pp)])

    @pl.when(g == pl.num_programs(1) - 1)
    def _():
        update([lambda h: kn_ref[0, :, h, :]], [lambda h: vn_ref[0, :, h, :]], [biasn_ref[0, 0]])
        o_ref[0] = acc_sc[...] / l_sc[...]


def _dsa_sample(q, k, v, qi, ki, wi, cache_k, cache_v, cache_kidx, layer, page_table, pp_idx=16, pp_att=4):
    bn, n_new, nh, dh = q.shape
    n_pages = page_table.shape[1]
    topk = min(TOPK_MAX, (n_pages * PAGE_SIZE + n_new) // 4)
    pp_idx = math.gcd(pp_idx, n_pages)
    pp_att = math.gcd(pp_att, n_pages)
    pt_flat = page_table.reshape(-1).astype(jnp.int32)
    padq = lambda a: jnp.pad(a, ((0, 0), (0, 0), (0, QPAD - n_new)) + ((0, 0),) * (a.ndim - 3))
    rows_i = IDX_HEADS * QPAD
    qi_rows = padq(qi.astype(BF16).transpose(0, 2, 1, 3)).reshape(bn, rows_i, IDX_DIM)
    wcol = jnp.broadcast_to(padq(wi.astype(F32).transpose(0, 2, 1)).reshape(bn, rows_i, 1), (bn, rows_i, PAGE_SIZE))
    padk = lambda a: jnp.pad(a.astype(F32), ((0, 0), (0, PAGE_SIZE - n_new), (0, 0)))
    ki_new = padk(ki)

    def page_spec_idx(j):
        return pl.BlockSpec((1, 1, PAGE_SIZE, IDX_DIM),
                            lambda b, g, pt, j=j: (layer, pt[b * n_pages + g * pp_idx + j], 0, 0))

    bias = pl.pallas_call(
        functools.partial(_sample_index_kernel, pp=pp_idx, n_new=n_new, topk=float(topk)),
        grid_spec=pltpu.PrefetchScalarGridSpec(
            num_scalar_prefetch=1,
            grid=(bn, n_pages // pp_idx),
            in_specs=[
                pl.BlockSpec((1, rows_i, IDX_DIM), lambda b, g, pt: (b, 0, 0)),
                pl.BlockSpec((1, rows_i, PAGE_SIZE), lambda b, g, pt: (b, 0, 0)),
                pl.BlockSpec((1, PAGE_SIZE, IDX_DIM), lambda b, g, pt: (b, 0, 0)),
            ] + [page_spec_idx(j) for j in range(pp_idx)],
            out_specs=pl.BlockSpec((1, n_pages + 1, QPAD, PAGE_SIZE), lambda b, g, pt: (b, 0, 0, 0)),
            scratch_shapes=[pltpu.VMEM((n_pages + 1, QPAD, PAGE_SIZE), jnp.int32)],
        ),
        out_shape=jax.ShapeDtypeStruct((bn, n_pages + 1, QPAD, PAGE_SIZE), F32),
        compiler_params=pltpu.CompilerParams(
            dimension_semantics=("arbitrary", "arbitrary"), vmem_limit_bytes=VMEM_LIMIT_BYTES),
    )(pt_flat, qi_rows, wcol, ki_new, *([cache_kidx] * pp_idx))

    rows_a = nh * QPAD
    scale = (ATT_DIM ** -0.5) * math.log2(math.e)
    q_rows = padq((q * scale).astype(BF16).transpose(0, 2, 1, 3)).reshape(bn, rows_a, dh)
    padt = lambda a: jnp.pad(a.astype(F32), ((0, 0), (0, PAGE_SIZE - n_new), (0, 0), (0, 0)))
    k_new, v_new = padt(k), padt(v)

    def page_spec_att(j):
        return pl.BlockSpec((1, 1, PAGE_SIZE, nh, dh),
                            lambda b, g, pt, j=j: (layer, pt[b * n_pages + g * pp_att + j], 0, 0, 0))

    o_rows = pl.pallas_call(
        functools.partial(_sample_attn_kernel, pp=pp_att),
        grid_spec=pltpu.PrefetchScalarGridSpec(
            num_scalar_prefetch=1,
            grid=(bn, n_pages // pp_att),
            in_specs=[
                pl.BlockSpec((1, rows_a, dh), lambda b, g, pt: (b, 0, 0)),
                pl.BlockSpec((1, pp_att, QPAD, PAGE_SIZE), lambda b, g, pt: (b, g, 0, 0)),
                pl.BlockSpec((1, 1, QPAD, PAGE_SIZE), lambda b, g, pt: (b, n_pages, 0, 0)),
                pl.BlockSpec((1, PAGE_SIZE, nh, dh), lambda b, g, pt: (b, 0, 0, 0)),
                pl.BlockSpec((1, PAGE_SIZE, nh, dh), lambda b, g, pt: (b, 0, 0, 0)),
            ] + [page_spec_att(j) for j in range(pp_att)] * 2,
            out_specs=pl.BlockSpec((1, rows_a, dh), lambda b, g, pt: (b, 0, 0)),
            scratch_shapes=[pltpu.VMEM((rows_a, 1), F32), pltpu.VMEM((rows_a, 1), F32),
                            pltpu.VMEM((rows_a, dh), F32)],
        ),
        out_shape=jax.ShapeDtypeStruct((bn, rows_a, dh), F32),
        compiler_params=pltpu.CompilerParams(
            dimension_semantics=("arbitrary", "arbitrary"), vmem_limit_bytes=VMEM_LIMIT_BYTES),
    )(pt_flat, q_rows, bias, bias, k_new, v_new, *([cache_k] * pp_att), *([cache_v] * pp_att))
    return o_rows.reshape(bn, nh, QPAD, dh)[:, :, :n_new].transpose(0, 2, 1, 3)


HG_SUB = 16
HG_PAD_LOGIT = 30.0


def _hgrn_kernel(aq_ref, af_ref, ai_ref, lb_ref, s0_ref, tri_ref, o_ref, sfin_ref, st_sc, *, n_sub):
    c = pl.program_id(1)
    n_pairs = st_sc.shape[0]
    w = st_sc.shape[1]
    dk = w // 2

    @pl.when(c == 0)
    def _():
        st_sc[...] = jnp.zeros(st_sc.shape, F32)
        for p in range(n_pairs):
            for e in range(2):
                st_sc[p, e * dk:(e + 1) * dk, e * dk:(e + 1) * dk] = s0_ref[0, 2 * p + e]

    lb = lb_ref[...]
    f = lb + (1.0 - lb) * jax.nn.sigmoid(af_ref[0])
    g = jnp.log(f)
    kk = 1.0 - f
    g_hi = g.astype(BF16)
    g_lo = (g - g_hi.astype(F32)).astype(BF16)
    tri = tri_ref[...]
    b = jnp.dot(tri, g_hi, preferred_element_type=F32) + jnp.dot(tri, g_lo, preferred_element_type=F32)
    q = aq_ref[0]
    iv = ai_ref[0]

    r = lax.broadcasted_iota(jnp.int32, (w, w), 0)
    cc = lax.broadcasted_iota(jnp.int32, (w, w), 1)
    same_head = (r // dk) == (cc // dk)
    ones_bd = jnp.where(same_head, 1.0, 0.0).astype(BF16)
    mask_bd = jnp.where(same_head, 1.0, 0.0)
    trow = lax.broadcasted_iota(jnp.int32, (HG_SUB, w), 0)

    for p in range(n_pairs):
        lanes = slice(p * w, (p + 1) * w)
        st = st_sc[p]
        for j in range(n_sub):
            rows = slice(j * HG_SUB, (j + 1) * HG_SUB)
            b_blk = b[rows, lanes]
            brel = b_blk if j == 0 else b_blk - b[j * HG_SUB - 1:j * HG_SUB, lanes]
            q_blk, kk_blk, i_blk = q[rows, lanes], kk[rows, lanes], iv[rows, lanes]
            o_blk = lax.dot_general((q_blk * jnp.exp(brel)).astype(BF16), st.astype(BF16),
                                    (((1,), (1,)), ((), ())), preferred_element_type=F32)
            for s in range(HG_SUB):
                e = jnp.where(trow >= s, brel - brel[s:s + 1, :], NEG_BIG)
                x = q_blk * jnp.exp(e) * kk_blk[s:s + 1, :]
                a_s = jnp.dot(x.astype(BF16), ones_bd, preferred_element_type=F32)
                o_blk = o_blk + a_s * i_blk[s:s + 1, :]
            o_ref[0, rows, lanes] = o_blk
            b_last = brel[HG_SUB - 1:HG_SUB, :]
            kt = kk_blk * jnp.exp(b_last - brel)
            upd = lax.dot_general(i_blk.astype(BF16), kt.astype(BF16), (((0,), (0,)), ((), ())),
                                  preferred_element_type=F32)
            st = jnp.exp(b_last) * st + mask_bd * upd
        st_sc[p] = st

    @pl.when(c == pl.num_programs(1) - 1)
    def _():
        for p in range(n_pairs):
            for e in range(2):
                sfin_ref[0, 2 * p + e] = st_sc[p, e * dk:(e + 1) * dk, e * dk:(e + 1) * dk]


def _hgrn2(z, col0, lb, s0, bn, length, width, t_blk=128):
    dk = width // HG_HEADS
    lpad = -(-length // HG_SUB) * HG_SUB
    t_blk = min(t_blk, lpad)
    assert lpad % t_blk == 0 and col0 % width == 0
    z3 = z.reshape(bn, length, z.shape[1])
    cb = col0 // width
    if lpad != length:
        z3 = jnp.pad(z3[..., col0:col0 + 3 * width], ((0, 0), (0, lpad - length), (0, 0)),
                     constant_values=HG_PAD_LOGIT)
        cb = 0
    tri = jnp.tril(jnp.ones((t_blk, t_blk), F32)).astype(BF16)
    s0t = jnp.swapaxes(s0.astype(F32), -1, -2)
    col = lambda k: pl.BlockSpec((1, t_blk, width), lambda b, c, k=k: (b, c, cb + k))
    o, sfin_t = pl.pallas_call(
        functools.partial(_hgrn_kernel, n_sub=t_blk // HG_SUB),
        grid=(bn, lpad // t_blk),
        in_specs=[
            col(0), col(1), col(2),
            pl.BlockSpec((1, width), lambda b, c: (0, 0)),
            pl.BlockSpec((1, HG_HEADS, dk, dk), lambda b, c: (b, 0, 0, 0)),
            pl.BlockSpec((t_blk, t_blk), lambda b, c: (0, 0)),
        ],
        out_specs=[
            pl.BlockSpec((1, t_blk, width), lambda b, c: (b, c, 0)),
            pl.BlockSpec((1, HG_HEADS, dk, dk), lambda b, c: (b, 0, 0, 0)),
        ],
        out_shape=[jax.ShapeDtypeStruct((bn, lpad, width), F32),
                   jax.ShapeDtypeStruct((bn, HG_HEADS, dk, dk), F32)],
        scratch_shapes=[pltpu.VMEM((HG_HEADS // 2, 2 * dk, 2 * dk), F32)],
        compiler_params=pltpu.CompilerParams(
            dimension_semantics=("arbitrary", "arbitrary"), vmem_limit_bytes=VMEM_LIMIT_BYTES),
    )(z3, z3, z3, lb.reshape(1, width).astype(F32), s0t, tri)
    return o[:, :length], jnp.swapaxes(sfin_t, -1, -2)


def _gmlp_kernel(bu_ref, bv_ref, g_ref, w_ref, bias_ref, out_ref, vn_ref):
    u = jax.nn.gelu(bu_ref[...])
    v = jax.nn.gelu(bv_ref[...])
    vn = v * lax.rsqrt(jnp.mean(v * v, axis=-1, keepdims=True) + RMS_EPS) * g_ref[...]
    vn_ref[...] = vn
    vb = vn.astype(BF16)
    width = vn.shape[1]
    dg = width // GM_GROUPS
    group = lax.broadcasted_iota(jnp.int32, vn.shape, 1) // dg
    mixed = bias_ref[...]
    for gi in range(GM_GROUPS):
        mixed = mixed + jnp.where(group == gi, jnp.dot(w_ref[gi], vb, preferred_element_type=F32), 0.0)
    out_ref[...] = u * mixed


def _gmlp(z, col_u, g_v, w_s, b_s, bn, length, width):
    m = bn * length
    w_causal = w_s * jnp.tril(jnp.ones((GM_CHUNK, GM_CHUNK), w_s.dtype))
    dg = width // GM_GROUPS
    if length % GM_CHUNK == 0:
        w_blk = w_causal
        bias = jnp.repeat(b_s.T, dg, axis=1)
    else:
        assert GM_CHUNK % length == 0 and m % GM_CHUNK == 0
        reps = GM_CHUNK // length
        w_blk = jnp.einsum('ab,gts->gatbs', jnp.eye(reps, dtype=w_s.dtype), w_causal[:, :length, :length])
        w_blk = w_blk.reshape(GM_GROUPS, GM_CHUNK, GM_CHUNK)
        bias = jnp.tile(jnp.repeat(b_s.T[:length], dg, axis=1), (reps, 1))
    assert col_u % width == 0
    cb = col_u // width
    return pl.pallas_call(
        _gmlp_kernel,
        grid=(m // GM_CHUNK,),
        in_specs=[
            pl.BlockSpec((GM_CHUNK, width), lambda i: (i, cb)),
            pl.BlockSpec((GM_CHUNK, width), lambda i: (i, cb + 1)),
            pl.BlockSpec((1, width), lambda i: (0, 0)),
            pl.BlockSpec((GM_GROUPS, GM_CHUNK, GM_CHUNK), lambda i: (0, 0, 0)),
            pl.BlockSpec((GM_CHUNK, width), lambda i: (0, 0)),
        ],
        out_specs=[pl.BlockSpec((GM_CHUNK, width), lambda i: (i, 0))] * 2,
        out_shape=[jax.ShapeDtypeStruct((m, width), F32)] * 2,
        compiler_params=pltpu.CompilerParams(
            dimension_semantics=("arbitrary",), vmem_limit_bytes=VMEM_LIMIT_BYTES),
    )(z, z, g_v.reshape(1, width).astype(F32), w_blk.astype(BF16), bias.astype(F32))


def _xattn_kernel(q_ref, mk_ref, mv_ref, o_ref, *, n_heads):
    dh = q_ref.shape[-1] // n_heads
    scale = dh ** -0.5
    for h in range(n_heads):
        cols = slice(h * dh, (h + 1) * dh)
        qh = (q_ref[0, :, cols] * scale).astype(BF16)
        s = lax.dot_general(qh, mk_ref[0, :, cols].astype(BF16), (((1,), (1,)), ((), ())),
                            preferred_element_type=F32)
        p = jnp.exp(s - s.max(axis=-1, keepdims=True))
        o = jnp.dot(p.astype(BF16), mv_ref[0, :, cols].astype(BF16), preferred_element_type=F32)
        o_ref[0, :, cols] = o / p.sum(axis=-1, keepdims=True)


def _xattn(q, mk, mv, tm=512):
    bn, length, width = q.shape
    mem = mk.shape[1]
    tm = min(tm, length)
    assert length % tm == 0
    return pl.pallas_call(
        functools.partial(_xattn_kernel, n_heads=X_HEADS),
        grid=(bn, length // tm),
        in_specs=[
            pl.BlockSpec((1, tm, width), lambda b, i: (b, i, 0)),
            pl.BlockSpec((1, mem, width), lambda b, i: (b, 0, 0)),
            pl.BlockSpec((1, mem, width), lambda b, i: (b, 0, 0)),
        ],
        out_specs=pl.BlockSpec((1, tm, width), lambda b, i: (b, i, 0)),
        out_shape=jax.ShapeDtypeStruct((bn, length, width), F32),
        compiler_params=pltpu.CompilerParams(
            dimension_semantics=("arbitrary", "arbitrary"), vmem_limit_bytes=VMEM_LIMIT_BYTES),
    )(q, mk.reshape(bn, mem, width), mv.reshape(bn, mem, width))


def _merge_matmul_kernel(ohg_ref, gm_ref, att_ref, g1_ref, g2_ref, g3a_ref, g3b_ref, ghg_ref, w_ref, r_ref,
                         o_ref, h_sc, *, hg_dim):
    @pl.when(pl.program_id(1) == 0)
    def _():
        o = ohg_ref[...]
        wh = o.shape[1]
        r = lax.broadcasted_iota(jnp.int32, (wh, wh), 0) // hg_dim
        c = lax.broadcasted_iota(jnp.int32, (wh, wh), 1) // hg_dim
        ones_bd = jnp.where(r == c, 1.0, 0.0).astype(BF16)
        sq = o * o
        hi = sq.astype(BF16)
        lo = (sq - hi.astype(F32)).astype(BF16)
        ms = (jnp.dot(hi, ones_bd, preferred_element_type=F32)
              + jnp.dot(lo, ones_bd, preferred_element_type=F32)) * (1.0 / hg_dim)
        y = o * lax.rsqrt(ms + RMS_EPS) * ghg_ref[...]
        wg = gm_ref.shape[1]
        wa = g3a_ref.shape[1]
        h_sc[:, 0:wh] = (y * jax.nn.silu(g1_ref[...])).astype(BF16)
        h_sc[:, wh:wh + wg] = (gm_ref[...] * jax.nn.silu(g2_ref[...])).astype(BF16)
        h_sc[:, wh + wg:wh + wg + wa] = (att_ref[:, 0:wa] * jax.nn.silu(g3a_ref[...])).astype(BF16)
        h_sc[:, wh + wg + wa:] = (att_ref[:, wa:] * jax.nn.silu(g3b_ref[...])).astype(BF16)

    o_ref[...] = jnp.dot(h_sc[...], w_ref[...], preferred_element_type=F32) + r_ref[...]


def _merge_matmul(x, z, o_hg, gm_out, att, g_hg, w_out, cols, tm=512, tn=512):
    m, d = x.shape
    wh, wg, wa = o_hg.shape[1], gm_out.shape[1], att.shape[1]
    assert wh == wg and wa == 2 * wh and all(c % wh == 0 for c in cols)
    k = wh + wg + wa
    tm = min(tm, m)
    tn = min(tn, d)
    c1, c2, c3 = (c // wh for c in cols)
    row = lambda width: pl.BlockSpec((tm, width), lambda i, j: (i, 0))
    zcol = lambda cidx: pl.BlockSpec((tm, wh), lambda i, j, cidx=cidx: (i, cidx))
    return pl.pallas_call(
        functools.partial(_merge_matmul_kernel, hg_dim=wh // HG_HEADS),
        grid=(m // tm, d // tn),
        in_specs=[row(wh), row(wg), row(wa), zcol(c1), zcol(c2), zcol(c3), zcol(c3 + 1),
                  pl.BlockSpec((1, wh), lambda i, j: (0, 0)),
                  pl.BlockSpec((k, tn), lambda i, j: (0, j)),
                  pl.BlockSpec((tm, tn), lambda i, j: (i, j))],
        out_specs=pl.BlockSpec((tm, tn), lambda i, j: (i, j)),
        out_shape=jax.ShapeDtypeStruct((m, d), F32),
        scratch_shapes=[pltpu.VMEM((tm, k), BF16)],
        compiler_params=pltpu.CompilerParams(
            dimension_semantics=("arbitrary", "arbitrary"), vmem_limit_bytes=VMEM_LIMIT_BYTES),
    )(o_hg, gm_out, att, z, z, z, z, g_hg.reshape(1, wh).astype(F32), w_out.astype(BF16), x)


def _rmsnorm_kernel(x_ref, g_ref, o_ref):
    x = x_ref[...]
    o_ref[...] = x * lax.rsqrt(jnp.mean(x * x, axis=-1, keepdims=True) + RMS_EPS) * g_ref[...]


def _rmsnorm(x, g, tm=512):
    m, d = x.shape
    tm = min(tm, m)
    return pl.pallas_call(
        _rmsnorm_kernel,
        grid=(m // tm,),
        in_specs=[pl.BlockSpec((tm, d), lambda i: (i, 0)), pl.BlockSpec((1, d), lambda i: (0, 0))],
        out_specs=pl.BlockSpec((tm, d), lambda i: (i, 0)),
        out_shape=jax.ShapeDtypeStruct((m, d), F32),
        compiler_params=pltpu.CompilerParams(dimension_semantics=("arbitrary",), vmem_limit_bytes=VMEM_LIMIT_BYTES),
    )(x, g.reshape(1, d).astype(F32))


def _mix_layer(x, layer, lb, s0, params, dsa_fn):
    g_mix, w_in, g_hg, g_gm, w_spatial, b_spatial, w_out = params
    bn, length, d = x.shape
    m = bn * length
    mix = w_out.shape[0]
    hgw, gmw, attw = mix // 4, mix // 4, mix // 2
    n_in = w_in.shape[1]
    n_pad = -(-n_in // 512) * 512
    x2 = x.reshape(m, d)
    z = _norm_matmul(x2, jnp.pad(w_in, ((0, 0), (0, n_pad - n_in))), g=g_mix, tm=1024)
    c_hg, c_gm, c_att = 0, 4 * hgw, 4 * hgw + 3 * gmw
    c_idx = c_att + 4 * attw
    o_hg, s_fin = _hgrn2(z, c_hg, lb, s0, bn, length, hgw)
    gm_out, vn = _gmlp(z, c_gm, g_gm, w_spatial, b_spatial, bn, length, gmw)
    sec = lambda c, wdt: z[:, c:c + wdt].reshape(bn, length, wdt)
    q = sec(c_att, attw).reshape(bn, length, ATT_HEADS, ATT_DIM)
    k = sec(c_att + attw, attw).reshape(bn, length, ATT_HEADS, ATT_DIM)
    v = sec(c_att + 2 * attw, attw).reshape(bn, length, ATT_HEADS, ATT_DIM)
    qi = sec(c_idx, IDX_HEADS * IDX_DIM).reshape(bn, length, IDX_HEADS, IDX_DIM)
    ki = sec(c_idx + IDX_HEADS * IDX_DIM, IDX_DIM)
    wi = sec(c_idx + IDX_HEADS * IDX_DIM + IDX_DIM, IDX_HEADS)
    att = dsa_fn(q, k, v, qi, ki, wi)
    y = _merge_matmul(x2, z, o_hg.reshape(m, hgw), gm_out, att.reshape(m, attw), g_hg, w_out,
                      (c_hg + 3 * hgw, c_gm + 2 * gmw, c_att + 3 * attw))
    return y.reshape(bn, length, d), k, v, ki, s_fin, vn.reshape(bn, length, gmw)


def _cross_layer(x, g, mk, mv, w_q, w_o):
    bn, length, d = x.shape
    m = bn * length
    x2 = x.reshape(m, d)
    q = _norm_matmul(x2, w_q, g=g).reshape(bn, length, -1)
    o = _xattn(q, mk, mv)
    return _norm_matmul(o.reshape(m, -1), w_o, res=x2).reshape(bn, length, d)


def kernel(x_prompt, x_sample, cache_k, cache_v, cache_kidx, cache_mem_k, cache_mem_v, state_hgrn, page_table, mem_prompt,
           g_mix, w_in, hg_lb, g_hg, g_gm, w_spatial, b_spatial, w_out, g_xattn, g_mem, w_xq, w_xk, w_xv, w_xo, g_final):
    depth, d_model, _ = w_in.shape
    hg_dim = w_out.shape[1] // 4 // HG_HEADS
    p_lb = jax.nn.softmax(hg_lb.astype(F32), axis=0)
    lbs = jnp.cumsum(p_lb, axis=0) - p_lb[0]
    xp, xs = x_prompt, x_sample
    bp, mem_len, _ = mem_prompt.shape
    xw = w_xk.shape[2]
    mem2 = mem_prompt.reshape(bp * mem_len, d_model)
    pk, pv, pki, phg, pmk, pmv = [], [], [], [], [], []
    sk, sv, ski, shg, sgv = [], [], [], [], []
    for l in range(depth):
        params = (g_mix[l], w_in[l], g_hg[l], g_gm[l], w_spatial[l], b_spatial[l], w_out[l])
        s0 = jnp.zeros((xp.shape[0], HG_HEADS, hg_dim, hg_dim), F32)
        xp, k_p, v_p, ki_p, s_fin, _ = _mix_layer(xp, l, lbs[l], s0, params, _dsa_prompt)
        mk = _norm_matmul(mem2, w_xk[l], g=g_mem[l]).reshape(bp, mem_len, X_HEADS, xw // X_HEADS)
        mv = _norm_matmul(mem2, w_xv[l], g=g_mem[l]).reshape(bp, mem_len, X_HEADS, xw // X_HEADS)
        xp = _cross_layer(xp, g_xattn[l], mk, mv, w_xq[l], w_xo[l])
        pk.append(k_p); pv.append(v_p); pki.append(ki_p); phg.append(s_fin); pmk.append(mk); pmv.append(mv)

        dsa_s = functools.partial(_dsa_sample, cache_k=cache_k, cache_v=cache_v, cache_kidx=cache_kidx,
                                  layer=l, page_table=page_table)
        xs, k_s, v_s, ki_s, s_new, vn_s = _mix_layer(xs, l, lbs[l], state_hgrn[l], params, dsa_s)
        xs = _cross_layer(xs, g_xattn[l], cache_mem_k[l], cache_mem_v[l], w_xq[l], w_xo[l])
        sk.append(k_s); sv.append(v_s); ski.append(ki_s); shg.append(s_new); sgv.append(vn_s)
    y_prompt = _rmsnorm(xp.reshape(-1, d_model), g_final).reshape(xp.shape)
    y_sample = _rmsnorm(xs.reshape(-1, d_model), g_final).reshape(xs.shape)
    return (y_prompt, y_sample,
            jnp.stack(pk), jnp.stack(pv), jnp.stack(pki), jnp.stack(phg), jnp.stack(pmk), jnp.stack(pmv),
            jnp.stack(sk), jnp.stack(sv), jnp.stack(ski), jnp.stack(shg), jnp.stack(sgv))
```

```python
import functools
import math

import jax
import jax.numpy as jnp
from jax import lax
from jax.experimental import pallas as pl
from jax.experimental.pallas import tpu as pltpu

F32 = jnp.float32
BF16 = jnp.bfloat16

HG_HEADS = 4
HG_CHUNK = 64
GM_GROUPS = 4
GM_CHUNK = 128
ATT_HEADS = 8
ATT_DIM = 64
IDX_HEADS = 8
IDX_DIM = 64
TOPK_MAX = 256
PAGE_SIZE = 128
X_HEADS = 4
RMS_EPS = 1e-6

LANES = 128
VMEM_LIMIT_BYTES = 48 * 1024 * 1024

INT_MIN = -(2 ** 31)
NEG_BIG = -1e30


def _norm_matmul_kernel(*refs, norm, has_res):
    if has_res:
        x_ref, g_ref, w_ref, r_ref, o_ref, h_sc = refs
    else:
        x_ref, g_ref, w_ref, o_ref, h_sc = refs

    @pl.when(pl.program_id(1) == 0)
    def _():
        x = x_ref[...]
        if norm:
            ms = jnp.mean(x * x, axis=-1, keepdims=True)
            x = x * lax.rsqrt(ms + RMS_EPS) * g_ref[...]
        h_sc[...] = x.astype(BF16)

    acc = jnp.dot(h_sc[...], w_ref[...], preferred_element_type=F32)
    if has_res:
        acc = acc + r_ref[...]
    o_ref[...] = acc


def _norm_matmul(x, w, g=None, res=None, tm=1024, tn=512):
    m, k = x.shape
    n = w.shape[1]
    tm = min(tm, m)
    tn = min(tn, n)
    assert m % tm == 0 and n % tn == 0
    norm = g is not None
    g2 = (g if norm else jnp.ones((k,), F32)).reshape(1, k).astype(F32)
    in_specs = [
        pl.BlockSpec((tm, k), lambda i, j: (i, 0)),
        pl.BlockSpec((1, k), lambda i, j: (0, 0)),
        pl.BlockSpec((k, tn), lambda i, j: (0, j)),
    ]
    args = [x, g2, w.astype(BF16)]
    if res is not None:
        in_specs.append(pl.BlockSpec((tm, tn), lambda i, j: (i, j)))
        args.append(res)
    return pl.pallas_call(
        functools.partial(_norm_matmul_kernel, norm=norm, has_res=res is not None),
        grid=(m // tm, n // tn),
        in_specs=in_specs,
        out_specs=pl.BlockSpec((tm, tn), lambda i, j: (i, j)),
        out_shape=jax.ShapeDtypeStruct((m, n), F32),
        scratch_shapes=[pltpu.VMEM((tm, k), BF16)],
        compiler_params=pltpu.CompilerParams(
            dimension_semantics=("arbitrary", "arbitrary"), vmem_limit_bytes=VMEM_LIMIT_BYTES),
    )(*args)


def _dsa_prompt_kernel(qT_ref, qiT_ref, wiT_ref, k_ref, vT_ref, ki_ref, o_ref,
                       key_sc, bias_sc, m_sc, l_sc, acc_sc, *, topk):
    tq = qT_ref.shape[4] // 2
    ck = tq
    n_pairs = k_ref.shape[1]
    dh = vT_ref.shape[3]
    i = pl.program_id(1)
    n = i + 1
    row = lax.broadcasted_iota(jnp.int32, (ck, tq), 0)
    col = lax.broadcasted_iota(jnp.int32, (ck, tq), 1)

    def chunk(c):
        return pl.ds(pl.multiple_of(c * ck, ck), ck)

    def score_body(c, carry):
        ki_c = ki_ref[0, chunk(c), :]
        ss = [jnp.dot(ki_c, qiT_ref[0, 0, :, h * tq:(h + 1) * tq], preferred_element_type=F32)
              for h in range(IDX_HEADS)]
        acc = jnp.maximum(ss[0], 0.0) * wiT_ref[0, 0, 0:1, :]
        for h in range(1, IDX_HEADS):
            acc = acc + jnp.maximum(ss[h], 0.0) * wiT_ref[0, 0, h:h + 1, :]
        bits = pltpu.bitcast(acc, jnp.int32)
        key = bits ^ ((bits >> 31) & 0x7FFFFFFF)
        key = jnp.where(key == -1, 0, key)
        adm = (row + c * ck) <= (col + i * tq)
        key_sc[chunk(c), :] = jnp.where(adm, key, INT_MIN)
        return carry

    lax.fori_loop(0, n, score_body, 0)

    def bit_body(bi, carry):
        thr, n_gt = carry
        cand = thr + lax.shift_left(jnp.int32(1), 31 - bi)

        def cnt_body(c, a):
            ge = jnp.where(key_sc[chunk(c), :] >= cand, 1.0, 0.0)
            part = ge.reshape(4, ck // 4, tq).sum(axis=0)
            return a + part.reshape(ck // 32, 8, tq).sum(axis=0)

        a = lax.fori_loop(0, n, cnt_body, jnp.zeros((8, tq), F32))
        total = a.sum(axis=0, keepdims=True)
        ok = total >= topk
        return jnp.where(ok, cand, thr), jnp.where(ok, n_gt, total)

    thr, n_gt = lax.fori_loop(
        0, 32, bit_body, (jnp.full((1, tq), INT_MIN, jnp.int32), jnp.zeros((1, tq), F32)))

    need = topk - n_gt
    tri = jnp.where(row > col, 1.0, 0.0).astype(BF16)

    def bias_body(c, eq_before):
        kk = key_sc[chunk(c), :]
        eq = kk == thr
        eqf = jnp.where(eq, 1.0, 0.0)
        rank = jnp.dot(tri, eqf.astype(BF16), preferred_element_type=F32) + eq_before
        tie = jnp.where(rank < need, jnp.where(kk != INT_MIN, 0.0, NEG_BIG), NEG_BIG)
        bias_sc[chunk(c), :] = jnp.where(kk > thr, 0.0, jnp.where(eq, tie, NEG_BIG))
        return eq_before + eqf.sum(axis=0, keepdims=True)

    lax.fori_loop(0, n, bias_body, jnp.zeros((1, tq), F32))

    m_sc[...] = jnp.full(m_sc.shape, NEG_BIG, F32)
    l_sc[...] = jnp.zeros(l_sc.shape, F32)
    acc_sc[...] = jnp.zeros(acc_sc.shape, F32)

    def att_body(c, carry):
        bias = bias_sc[chunk(c), :]
        logits = [jnp.dot(k_ref[0, pr, chunk(c), :], qT_ref[0, 0, pr], preferred_element_type=F32)
                  for pr in range(n_pairs)]
        ps, alphas = [], []
        for h in range(2 * n_pairs):
            s = logits[h // 2][:, (h % 2) * tq:(h % 2 + 1) * tq] + bias
            m_old = m_sc[h:h + 1, :]
            m_new = jnp.maximum(m_old, s.max(axis=0, keepdims=True))
            alpha = jnp.exp2(m_old - m_new)
            p = jnp.exp2(s - m_new)
            l_sc[h:h + 1, :] = alpha * l_sc[h:h + 1, :] + p.sum(axis=0, keepdims=True)
            m_sc[h:h + 1, :] = m_new
            ps.append(p.astype(BF16))
            alphas.append(alpha)
        for h in range(2 * n_pairs):
            pv = jnp.dot(vT_ref[0, h, c], ps[h], preferred_element_type=F32)
            acc_sc[h] = alphas[h] * acc_sc[h] + pv
        return carry

    lax.fori_loop(0, n, att_body, 0)
    for h in range(2 * n_pairs):
        o_ref[0, 0, h * dh:(h + 1) * dh, :] = acc_sc[h] / l_sc[h:h + 1, :]


def _dsa_prompt(q, k, v, qi, ki, wi, tq=256):
    bn, length, nh, dh = q.shape
    tq = min(tq, length)
    nq = length // tq
    topk = min(TOPK_MAX, length // 4)
    scale = (ATT_DIM ** -0.5) * math.log2(math.e)
    qT = (q * scale).astype(BF16).reshape(bn, nq, tq, nh // 2, 2, dh).transpose(0, 1, 3, 4, 5, 2)
    eye2 = jnp.eye(2, dtype=BF16)
    qT_bd = jnp.einsum('bnpedt,ef->bnpedft', qT, eye2).reshape(bn, nq, nh // 2, 2 * dh, 2 * tq)
    qiT = qi.astype(BF16).reshape(bn, nq, tq, IDX_HEADS, IDX_DIM).transpose(0, 1, 4, 3, 2)
    qiT = qiT.reshape(bn, nq, IDX_DIM, IDX_HEADS * tq)
    wiT = wi.astype(F32).reshape(bn, nq, tq, IDX_HEADS).transpose(0, 1, 3, 2)
    k_pairs = k.astype(BF16).reshape(bn, length, nh // 2, 2 * dh).transpose(0, 2, 1, 3)
    vT = v.astype(BF16).reshape(bn, nq, tq, nh, dh).transpose(0, 3, 1, 4, 2)
    ki_b = ki.astype(BF16)
    oT = pl.pallas_call(
        functools.partial(_dsa_prompt_kernel, topk=float(topk)),
        grid=(bn, nq),
        in_specs=[
            pl.BlockSpec((1, 1, nh // 2, 2 * dh, 2 * tq), lambda b, i: (b, i, 0, 0, 0)),
            pl.BlockSpec((1, 1, IDX_DIM, IDX_HEADS * tq), lambda b, i: (b, i, 0, 0)),
            pl.BlockSpec((1, 1, IDX_HEADS, tq), lambda b, i: (b, i, 0, 0)),
            pl.BlockSpec((1, nh // 2, length, 2 * dh), lambda b, i: (b, 0, 0, 0)),
            pl.BlockSpec((1, nh, nq, dh, tq), lambda b, i: (b, 0, 0, 0, 0)),
            pl.BlockSpec((1, length, IDX_DIM), lambda b, i: (b, 0, 0)),
        ],
        out_specs=pl.BlockSpec((1, 1, nh * dh, tq), lambda b, i: (b, i, 0, 0)),
        out_shape=jax.ShapeDtypeStruct((bn, nq, nh * dh, tq), F32),
        scratch_shapes=[
            pltpu.VMEM((length, tq), jnp.int32), pltpu.VMEM((length, tq), F32),
            pltpu.VMEM((nh, tq), F32), pltpu.VMEM((nh, tq), F32), pltpu.VMEM((nh, dh, tq), F32)],
        compiler_params=pltpu.CompilerParams(
            dimension_semantics=("arbitrary", "arbitrary"), vmem_limit_bytes=VMEM_LIMIT_BYTES),
    )(qT_bd, qiT, wiT, k_pairs, vT, ki_b)
    return oT.transpose(0, 1, 3, 2).reshape(bn, length, nh, dh)


QPAD = 8


def _order_key(score):
    bits = pltpu.bitcast(score, jnp.int32)
    key = bits ^ ((bits >> 31) & 0x7FFFFFFF)
    return jnp.where(key == -1, 0, key)


def _sample_index_kernel(pt_ref, qi_ref, wcol_ref, kin_ref, *rest, pp, n_new, topk):
    page_refs = rest[:pp]
    bias_ref, key_sc = rest[pp], rest[pp + 1]
    g = pl.program_id(1)
    n_pages = pl.num_programs(1) * pp
    qi = qi_ref[0]
    wcol = wcol_ref[0]

    def page_keys(page):
        s = lax.dot_general(qi, page.astype(BF16), (((1,), (1,)), ((), ())), preferred_element_type=F32)
        sc = (jnp.maximum(s, 0.0) * wcol).reshape(IDX_HEADS, QPAD, PAGE_SIZE).sum(axis=0)
        return _order_key(sc)

    for j in range(pp):
        key_sc[g * pp + j] = page_keys(page_refs[j][0, 0])

    @pl.when(g == pl.num_programs(1) - 1)
    def _():
        rowq = lax.broadcasted_iota(jnp.int32, (QPAD, PAGE_SIZE), 0)
        lane = lax.broadcasted_iota(jnp.int32, (QPAD, PAGE_SIZE), 1)
        adm_new = (lane <= rowq) & (lane < n_new)
        key_sc[n_pages] = jnp.where(adm_new, page_keys(kin_ref[0]), INT_MIN)
        n_all = n_pages + 1

        def count_ge(cand):
            cand_b = jnp.broadcast_to(cand, (QPAD, PAGE_SIZE))
            a = lax.fori_loop(
                0, n_all, lambda p, a: a + jnp.where(key_sc[p] >= cand_b, 1.0, 0.0),
                jnp.zeros((QPAD, PAGE_SIZE), F32), unroll=4)
            return a.sum(axis=1, keepdims=True)

        def bit_body(bi, carry):
            thr, n_gt = carry
            cand = thr + lax.shift_left(jnp.int32(1), 31 - bi)
            total = count_ge(cand)
            ok = total >= topk
            return jnp.where(ok, cand, thr), jnp.where(ok, n_gt, total)

        thr, n_gt = lax.fori_loop(
            0, 32, bit_body, (jnp.full((QPAD, 1), INT_MIN, jnp.int32), jnp.zeros((QPAD, 1), F32)))
        n_ge = count_ge(thr)
        thr_b = jnp.broadcast_to(thr, (QPAD, PAGE_SIZE))

        def fast_body(p, carry):
            kk = key_sc[p]
            bias_ref[0, p] = jnp.where(kk >= thr_b, jnp.where(kk != INT_MIN, 0.0, NEG_BIG), NEG_BIG)
            return carry

        lax.fori_loop(0, n_all, fast_body, 0)

        @pl.when(jnp.max(n_ge) > topk)
        def _():
            need = jnp.broadcast_to(topk - n_gt, (QPAD, PAGE_SIZE))
            r = lax.broadcasted_iota(jnp.int32, (PAGE_SIZE, PAGE_SIZE), 0)
            c = lax.broadcasted_iota(jnp.int32, (PAGE_SIZE, PAGE_SIZE), 1)
            before = jnp.where(r < c, 1.0, 0.0).astype(BF16)

            def tie_body(p, eq_before):
                kk = key_sc[p]
                eq = kk == thr_b
                eqf = jnp.where(eq, 1.0, 0.0)
                rank = jnp.dot(eqf.astype(BF16), before, preferred_element_type=F32) + eq_before
                tie = jnp.where(rank < need, jnp.where(kk != INT_MIN, 0.0, NEG_BIG), NEG_BIG)
                bias_ref[0, p] = jnp.where(kk > thr_b, 0.0, jnp.where(eq, tie, NEG_BIG))
                return eq_before + eqf.sum(axis=1, keepdims=True)

            lax.fori_loop(0, n_all, tie_body, jnp.zeros((QPAD, 1), F32))


def _sample_attn_kernel(pt_ref, q_ref, bias_ref, biasn_ref, kn_ref, vn_ref, *rest, pp):
    k_refs, v_refs = rest[:pp], rest[pp:2 * pp]
    o_ref, m_sc, l_sc, acc_sc = rest[2 * pp:]
    g = pl.program_id(1)
    qbd = q_ref[0]

    @pl.when(g == 0)
    def _():
        m_sc[...] = jnp.full(m_sc.shape, NEG_BIG, F32)
        l_sc[...] = jnp.zeros(l_sc.shape, F32)
        acc_sc[...] = jnp.zeros(acc_sc.shape, F32)

    def update(ks, vs, biases):
        width = PAGE_SIZE * len(ks)
        logits = jnp.concatenate(
            [lax.dot_general(qbd, kp.astype(BF16), (((1,), (1,)), ((), ())), preferred_element_type=F32) for kp in ks],
            axis=1)
        bias = jnp.concatenate(biases, axis=1)
        s = (logits.reshape(ATT_HEADS, QPAD, width) + bias[None]).reshape(ATT_HEADS * QPAD, width)
        m_old = m_sc[...]
        m_new = jnp.maximum(m_old, s.max(axis=1, keepdims=True))
        alpha = jnp.exp2(m_old - m_new)
        p = jnp.exp2(s - m_new)
        l_sc[...] = alpha * l_sc[...] + p.sum(axis=1, keepdims=True)
        m_sc[...] = m_new
        pv = jnp.zeros(acc_sc.shape, F32)
        for j, vp in enumerate(vs):
            pv = pv + jnp.dot(p[:, j * PAGE_SIZE:(j + 1) * PAGE_SIZE].astype(BF16), vp.astype(BF16),
                              preferred_element_type=F32)
        acc_sc[...] = alpha * acc_sc[...] + pv

    update([r[0, 0] for r in k_refs], [r[0, 0] for r in v_refs], [bias_ref[0, j] for j in range(pp)])

    @pl.when(g == pl.num_programs(1) - 1)
    def _():
        update([kn_ref[0]], [vn_ref[0]], [biasn_ref[0, 0]])
        o_ref[0] = acc_sc[...] / l_sc[...]


def _dsa_sample(q, k, v, qi, ki, wi, cache_k, cache_v, cache_kidx, layer, page_table, pp_idx=16, pp_att=8):
    bn, n_new, nh, dh = q.shape
    n_pages = page_table.shape[1]
    depth, n_pool = cache_k.shape[:2]
    topk = min(TOPK_MAX, (n_pages * PAGE_SIZE + n_new) // 4)
    pp_idx = math.gcd(pp_idx, n_pages)
    pp_att = math.gcd(pp_att, n_pages)
    pt_flat = page_table.reshape(-1).astype(jnp.int32)
    padq = lambda a: jnp.pad(a, ((0, 0), (0, 0), (0, QPAD - n_new)) + ((0, 0),) * (a.ndim - 3))
    rows_i = IDX_HEADS * QPAD
    qi_rows = padq(qi.astype(BF16).transpose(0, 2, 1, 3)).reshape(bn, rows_i, IDX_DIM)
    wcol = jnp.broadcast_to(padq(wi.astype(F32).transpose(0, 2, 1)).reshape(bn, rows_i, 1), (bn, rows_i, PAGE_SIZE))
    padk = lambda a: jnp.pad(a.astype(F32), ((0, 0), (0, PAGE_SIZE - n_new), (0, 0)))
    ki_new = padk(ki)

    def page_spec_idx(j):
        return pl.BlockSpec((1, 1, PAGE_SIZE, IDX_DIM),
                            lambda b, g, pt, j=j: (layer, pt[b * n_pages + g * pp_idx + j], 0, 0))

    bias = pl.pallas_call(
        functools.partial(_sample_index_kernel, pp=pp_idx, n_new=n_new, topk=float(topk)),
        grid_spec=pltpu.PrefetchScalarGridSpec(
            num_scalar_prefetch=1,
            grid=(bn, n_pages // pp_idx),
            in_specs=[
                pl.BlockSpec((1, rows_i, IDX_DIM), lambda b, g, pt: (b, 0, 0)),
                pl.BlockSpec((1, rows_i, PAGE_SIZE), lambda b, g, pt: (b, 0, 0)),
                pl.BlockSpec((1, PAGE_SIZE, IDX_DIM), lambda b, g, pt: (b, 0, 0)),
            ] + [page_spec_idx(j) for j in range(pp_idx)],
            out_specs=pl.BlockSpec((1, n_pages + 1, QPAD, PAGE_SIZE), lambda b, g, pt: (b, 0, 0, 0)),
            scratch_shapes=[pltpu.VMEM((n_pages + 1, QPAD, PAGE_SIZE), jnp.int32)],
        ),
        out_shape=jax.ShapeDtypeStruct((bn, n_pages + 1, QPAD, PAGE_SIZE), F32),
        compiler_params=pltpu.CompilerParams(
            dimension_semantics=("arbitrary", "arbitrary"), vmem_limit_bytes=VMEM_LIMIT_BYTES),
    )(pt_flat, qi_rows, wcol, ki_new, *([cache_kidx] * pp_idx))

    width = nh * dh
    rows_a = nh * QPAD
    scale = (ATT_DIM ** -0.5) * math.log2(math.e)
    q_rows = padq((q * scale).astype(BF16).transpose(0, 2, 1, 3))
    q_bd = jnp.einsum('bhjd,hg->bhjgd', q_rows, jnp.eye(nh, dtype=BF16)).reshape(bn, rows_a, width)
    k_new = padk(k.reshape(bn, n_new, width))
    v_new = padk(v.reshape(bn, n_new, width))
    ck = cache_k.reshape(depth, n_pool, PAGE_SIZE, width)
    cv = cache_v.reshape(depth, n_pool, PAGE_SIZE, width)

    def page_spec_att(j):
        return pl.BlockSpec((1, 1, PAGE_SIZE, width),
                            lambda b, g, pt, j=j: (layer, pt[b * n_pages + g * pp_att + j], 0, 0))

    o_rows = pl.pallas_call(
        functools.partial(_sample_attn_kernel, pp=pp_att),
        grid_spec=pltpu.PrefetchScalarGridSpec(
            num_scalar_prefetch=1,
            grid=(bn, n_pages // pp_att),
            in_specs=[
                pl.BlockSpec((1, rows_a, width), lambda b, g, pt: (b, 0, 0)),
                pl.BlockSpec((1, pp_att, QPAD, PAGE_SIZE), lambda b, g, pt: (b, g, 0, 0)),
                pl.BlockSpec((1, 1, QPAD, PAGE_SIZE), lambda b, g, pt: (b, n_pages, 0, 0)),
                pl.BlockSpec((1, PAGE_SIZE, width), lambda b, g, pt: (b, 0, 0)),
                pl.BlockSpec((1, PAGE_SIZE, width), lambda b, g, pt: (b, 0, 0)),
            ] + [page_spec_att(j) for j in range(pp_att)] * 2,
            out_specs=pl.BlockSpec((1, rows_a, width), lambda b, g, pt: (b, 0, 0)),
            scratch_shapes=[pltpu.VMEM((rows_a, 1), F32), pltpu.VMEM((rows_a, 1), F32),
                            pltpu.VMEM((rows_a, width), F32)],
        ),
        out_shape=jax.ShapeDtypeStruct((bn, rows_a, width), F32),
        compiler_params=pltpu.CompilerParams(
            dimension_semantics=("arbitrary", "arbitrary"), vmem_limit_bytes=VMEM_LIMIT_BYTES),
    )(pt_flat, q_bd, bias, bias, k_new, v_new, *([ck] * pp_att), *([cv] * pp_att))
    o = o_rows.reshape(bn, nh, QPAD, nh, dh)[:, :, :n_new]
    return jnp.einsum('bhjgd,hg->bjhd', o, jnp.eye(nh, dtype=F32))


HG_SUB = 16
HG_PAD_LOGIT = 30.0


def _hgrn_kernel(aq_ref, af_ref, ai_ref, lb_ref, s0_ref, tri_ref, o_ref, sfin_ref, st_sc, *, n_sub):
    c = pl.program_id(1)
    n_pairs = st_sc.shape[0]
    w = st_sc.shape[1]
    dk = w // 2

    @pl.when(c == 0)
    def _():
        st_sc[...] = jnp.zeros(st_sc.shape, F32)
        for p in range(n_pairs):
            for e in range(2):
                st_sc[p, e * dk:(e + 1) * dk, e * dk:(e + 1) * dk] = s0_ref[0, 2 * p + e]

    lb = lb_ref[...]
    f = lb + (1.0 - lb) * jax.nn.sigmoid(af_ref[0])
    g = jnp.log(f)
    kk = 1.0 - f
    g_hi = g.astype(BF16)
    g_lo = (g - g_hi.astype(F32)).astype(BF16)
    tri = tri_ref[...]
    b = jnp.dot(tri, g_hi, preferred_element_type=F32) + jnp.dot(tri, g_lo, preferred_element_type=F32)
    q = aq_ref[0]
    iv = ai_ref[0]

    r = lax.broadcasted_iota(jnp.int32, (w, w), 0)
    cc = lax.broadcasted_iota(jnp.int32, (w, w), 1)
    same_head = (r // dk) == (cc // dk)
    ones_bd = jnp.where(same_head, 1.0, 0.0).astype(BF16)
    mask_bd = jnp.where(same_head, 1.0, 0.0)
    trow = lax.broadcasted_iota(jnp.int32, (HG_SUB, w), 0)

    for p in range(n_pairs):
        lanes = slice(p * w, (p + 1) * w)
        q_dec, intra, upd, dec = [], [], [], []
        for j in range(n_sub):
            rows = slice(j * HG_SUB, (j + 1) * HG_SUB)
            b_blk = b[rows, lanes]
            brel = b_blk if j == 0 else b_blk - b[j * HG_SUB - 1:j * HG_SUB, lanes]
            q_blk, kk_blk, i_blk = q[rows, lanes], kk[rows, lanes], iv[rows, lanes]
            q_dec.append((q_blk * jnp.exp(brel)).astype(BF16))
            xs = []
            for s in range(HG_SUB):
                e = jnp.where(trow >= s, brel - brel[s:s + 1, :], NEG_BIG)
                xs.append((q_blk * jnp.exp(e) * kk_blk[s:s + 1, :]).astype(BF16))
            a_all = jnp.dot(jnp.concatenate(xs, axis=0), ones_bd, preferred_element_type=F32)
            o_in = a_all[0:HG_SUB] * i_blk[0:1, :]
            for s in range(1, HG_SUB):
                o_in = o_in + a_all[s * HG_SUB:(s + 1) * HG_SUB] * i_blk[s:s + 1, :]
            intra.append(o_in)
            b_last = brel[HG_SUB - 1:HG_SUB, :]
            kt = kk_blk * jnp.exp(b_last - brel)
            upd.append(mask_bd * lax.dot_general(i_blk.astype(BF16), kt.astype(BF16), (((0,), (0,)), ((), ())),
                                                 preferred_element_type=F32))
            dec.append(jnp.exp(b_last))
        st = st_sc[p]
        for j in range(n_sub):
            rows = slice(j * HG_SUB, (j + 1) * HG_SUB)
            o_ref[0, rows, lanes] = intra[j] + lax.dot_general(
                q_dec[j], st.astype(BF16), (((1,), (1,)), ((), ())), preferred_element_type=F32)
            st = dec[j] * st + upd[j]
        st_sc[p] = st

    @pl.when(c == pl.num_programs(1) - 1)
    def _():
        for p in range(n_pairs):
            for e in range(2):
                sfin_ref[0, 2 * p + e] = st_sc[p, e * dk:(e + 1) * dk, e * dk:(e + 1) * dk]


def _hgrn2(z, col0, lb, s0, bn, length, width, t_blk=128):
    dk = width // HG_HEADS
    lpad = -(-length // HG_SUB) * HG_SUB
    t_blk = min(t_blk, lpad)
    assert lpad % t_blk == 0 and col0 % width == 0
    z3 = z.reshape(bn, length, z.shape[1])
    cb = col0 // width
    if lpad != length:
        z3 = jnp.pad(z3[..., col0:col0 + 3 * width], ((0, 0), (0, lpad - length), (0, 0)),
                     constant_values=HG_PAD_LOGIT)
        cb = 0
    tri = jnp.tril(jnp.ones((t_blk, t_blk), F32)).astype(BF16)
    s0t = jnp.swapaxes(s0.astype(F32), -1, -2)
    col = lambda k: pl.BlockSpec((1, t_blk, width), lambda b, c, k=k: (b, c, cb + k))
    o, sfin_t = pl.pallas_call(
        functools.partial(_hgrn_kernel, n_sub=t_blk // HG_SUB),
        grid=(bn, lpad // t_blk),
        in_specs=[
            col(0), col(1), col(2),
            pl.BlockSpec((1, width), lambda b, c: (0, 0)),
            pl.BlockSpec((1, HG_HEADS, dk, dk), lambda b, c: (b, 0, 0, 0)),
            pl.BlockSpec((t_blk, t_blk), lambda b, c: (0, 0)),
        ],
        out_specs=[
            pl.BlockSpec((1, t_blk, width), lambda b, c: (b, c, 0)),
            pl.BlockSpec((1, HG_HEADS, dk, dk), lambda b, c: (b, 0, 0, 0)),
        ],
        out_shape=[jax.ShapeDtypeStruct((bn, lpad, width), F32),
                   jax.ShapeDtypeStruct((bn, HG_HEADS, dk, dk), F32)],
        scratch_shapes=[pltpu.VMEM((HG_HEADS // 2, 2 * dk, 2 * dk), F32)],
        compiler_params=pltpu.CompilerParams(
            dimension_semantics=("arbitrary", "arbitrary"), vmem_limit_bytes=VMEM_LIMIT_BYTES),
    )(z3, z3, z3, lb.reshape(1, width).astype(F32), s0t, tri)
    return o[:, :length], jnp.swapaxes(sfin_t, -1, -2)


def _gmlp_kernel(bu_ref, bv_ref, g_ref, w_ref, bias_ref, out_ref, vn_ref):
    u = jax.nn.gelu(bu_ref[...])
    v = jax.nn.gelu(bv_ref[...])
    vn = v * lax.rsqrt(jnp.mean(v * v, axis=-1, keepdims=True) + RMS_EPS) * g_ref[...]
    vn_ref[...] = vn
    vb = vn.astype(BF16)
    width = vn.shape[1]
    dg = width // GM_GROUPS
    group = lax.broadcasted_iota(jnp.int32, vn.shape, 1) // dg
    mixed = bias_ref[...]
    for gi in range(GM_GROUPS):
        mixed = mixed + jnp.where(group == gi, jnp.dot(w_ref[gi], vb, preferred_element_type=F32), 0.0)
    out_ref[...] = u * mixed


def _gmlp(z, col_u, g_v, w_s, b_s, bn, length, width):
    m = bn * length
    w_causal = w_s * jnp.tril(jnp.ones((GM_CHUNK, GM_CHUNK), w_s.dtype))
    dg = width // GM_GROUPS
    if length % GM_CHUNK == 0:
        w_blk = w_causal
        bias = jnp.repeat(b_s.T, dg, axis=1)
    else:
        assert GM_CHUNK % length == 0 and m % GM_CHUNK == 0
        reps = GM_CHUNK // length
        w_blk = jnp.einsum('ab,gts->gatbs', jnp.eye(reps, dtype=w_s.dtype), w_causal[:, :length, :length])
        w_blk = w_blk.reshape(GM_GROUPS, GM_CHUNK, GM_CHUNK)
        bias = jnp.tile(jnp.repeat(b_s.T[:length], dg, axis=1), (reps, 1))
    assert col_u % width == 0
    cb = col_u // width
    return pl.pallas_call(
        _gmlp_kernel,
        grid=(m // GM_CHUNK,),
        in_specs=[
            pl.BlockSpec((GM_CHUNK, width), lambda i: (i, cb)),
            pl.BlockSpec((GM_CHUNK, width), lambda i: (i, cb + 1)),
            pl.BlockSpec((1, width), lambda i: (0, 0)),
            pl.BlockSpec((GM_GROUPS, GM_CHUNK, GM_CHUNK), lambda i: (0, 0, 0)),
            pl.BlockSpec((GM_CHUNK, width), lambda i: (0, 0)),
        ],
        out_specs=[pl.BlockSpec((GM_CHUNK, width), lambda i: (i, 0))] * 2,
        out_shape=[jax.ShapeDtypeStruct((m, width), F32)] * 2,
        compiler_params=pltpu.CompilerParams(
            dimension_semantics=("arbitrary",), vmem_limit_bytes=VMEM_LIMIT_BYTES),
    )(z, z, g_v.reshape(1, width).astype(F32), w_blk.astype(BF16), bias.astype(F32))


def _xattn_kernel(q_ref, mk_ref, mv_ref, o_ref, *, n_heads):
    dh = q_ref.shape[-1] // n_heads
    scale = dh ** -0.5
    for h in range(n_heads):
        cols = slice(h * dh, (h + 1) * dh)
        qh = (q_ref[0, :, cols] * scale).astype(BF16)
        s = lax.dot_general(qh, mk_ref[0, :, cols].astype(BF16), (((1,), (1,)), ((), ())),
                            preferred_element_type=F32)
        p = jnp.exp(s - s.max(axis=-1, keepdims=True))
        o = jnp.dot(p.astype(BF16), mv_ref[0, :, cols].astype(BF16), preferred_element_type=F32)
        o_ref[0, :, cols] = o / p.sum(axis=-1, keepdims=True)


def _xattn(q, mk, mv, tm=512):
    bn, length, width = q.shape
    mem = mk.shape[1]
    tm = min(tm, length)
    assert length % tm == 0
    return pl.pallas_call(
        functools.partial(_xattn_kernel, n_heads=X_HEADS),
        grid=(bn, length // tm),
        in_specs=[
            pl.BlockSpec((1, tm, width), lambda b, i: (b, i, 0)),
            pl.BlockSpec((1, mem, width), lambda b, i: (b, 0, 0)),
            pl.BlockSpec((1, mem, width), lambda b, i: (b, 0, 0)),
        ],
        out_specs=pl.BlockSpec((1, tm, width), lambda b, i: (b, i, 0)),
        out_shape=jax.ShapeDtypeStruct((bn, length, width), F32),
        compiler_params=pltpu.CompilerParams(
            dimension_semantics=("arbitrary", "arbitrary"), vmem_limit_bytes=VMEM_LIMIT_BYTES),
    )(q, mk.reshape(bn, mem, width), mv.reshape(bn, mem, width))


def _merge_matmul_kernel(ohg_ref, gm_ref, att_ref, g1_ref, g2_ref, g3a_ref, g3b_ref, ghg_ref, w_ref, r_ref,
                         o_ref, h_sc, *, hg_dim):
    @pl.when(pl.program_id(1) == 0)
    def _():
        o = ohg_ref[...]
        wh = o.shape[1]
        r = lax.broadcasted_iota(jnp.int32, (wh, wh), 0) // hg_dim
        c = lax.broadcasted_iota(jnp.int32, (wh, wh), 1) // hg_dim
        ones_bd = jnp.where(r == c, 1.0, 0.0).astype(BF16)
        sq = o * o
        hi = sq.astype(BF16)
        lo = (sq - hi.astype(F32)).astype(BF16)
        ms = (jnp.dot(hi, ones_bd, preferred_element_type=F32)
              + jnp.dot(lo, ones_bd, preferred_element_type=F32)) * (1.0 / hg_dim)
        y = o * lax.rsqrt(ms + RMS_EPS) * ghg_ref[...]
        wg = gm_ref.shape[1]
        wa = g3a_ref.shape[1]
        h_sc[:, 0:wh] = (y * jax.nn.silu(g1_ref[...])).astype(BF16)
        h_sc[:, wh:wh + wg] = (gm_ref[...] * jax.nn.silu(g2_ref[...])).astype(BF16)
        h_sc[:, wh + wg:wh + wg + wa] = (att_ref[:, 0:wa] * jax.nn.silu(g3a_ref[...])).astype(BF16)
        h_sc[:, wh + wg + wa:] = (att_ref[:, wa:] * jax.nn.silu(g3b_ref[...])).astype(BF16)

    o_ref[...] = jnp.dot(h_sc[...], w_ref[...], preferred_element_type=F32) + r_ref[...]


def _merge_matmul(x, z, o_hg, gm_out, att, g_hg, w_out, cols, tm=1024, tn=512):
    m, d = x.shape
    wh, wg, wa = o_hg.shape[1], gm_out.shape[1], att.shape[1]
    assert wh == wg and wa == 2 * wh and all(c % wh == 0 for c in cols)
    k = wh + wg + wa
    tm = min(tm, m)
    tn = min(tn, d)
    c1, c2, c3 = (c // wh for c in cols)
    row = lambda width: pl.BlockSpec((tm, width), lambda i, j: (i, 0))
    zcol = lambda cidx: pl.BlockSpec((tm, wh), lambda i, j, cidx=cidx: (i, cidx))
    return pl.pallas_call(
        functools.partial(_merge_matmul_kernel, hg_dim=wh // HG_HEADS),
        grid=(m // tm, d // tn),
        in_specs=[row(wh), row(wg), row(wa), zcol(c1), zcol(c2), zcol(c3), zcol(c3 + 1),
                  pl.BlockSpec((1, wh), lambda i, j: (0, 0)),
                  pl.BlockSpec((k, tn), lambda i, j: (0, j)),
                  pl.BlockSpec((tm, tn), lambda i, j: (i, j))],
        out_specs=pl.BlockSpec((tm, tn), lambda i, j: (i, j)),
        out_shape=jax.ShapeDtypeStruct((m, d), F32),
        scratch_shapes=[pltpu.VMEM((tm, k), BF16)],
        compiler_params=pltpu.CompilerParams(
            dimension_semantics=("arbitrary", "arbitrary"), vmem_limit_bytes=VMEM_LIMIT_BYTES),
    )(o_hg, gm_out, att, z, z, z, z, g_hg.reshape(1, wh).astype(F32), w_out.astype(BF16), x)


def _rmsnorm_kernel(x_ref, g_ref, o_ref):
    x = x_ref[...]
    o_ref[...] = x * lax.rsqrt(jnp.mean(x * x, axis=-1, keepdims=True) + RMS_EPS) * g_ref[...]


def _rmsnorm(x, g, tm=512):
    m, d = x.shape
    tm = min(tm, m)
    return pl.pallas_call(
        _rmsnorm_kernel,
        grid=(m // tm,),
        in_specs=[pl.BlockSpec((tm, d), lambda i: (i, 0)), pl.BlockSpec((1, d), lambda i: (0, 0))],
        out_specs=pl.BlockSpec((tm, d), lambda i: (i, 0)),
        out_shape=jax.ShapeDtypeStruct((m, d), F32),
        compiler_params=pltpu.CompilerParams(dimension_semantics=("arbitrary",), vmem_limit_bytes=VMEM_LIMIT_BYTES),
    )(x, g.reshape(1, d).astype(F32))


def _mix_layer(x, layer, lb, s0, params, dsa_fn):
    g_mix, w_in, g_hg, g_gm, w_spatial, b_spatial, w_out = params
    bn, length, d = x.shape
    m = bn * length
    mix = w_out.shape[0]
    hgw, gmw, attw = mix // 4, mix // 4, mix // 2
    n_in = w_in.shape[1]
    n_pad = -(-n_in // 512) * 512
    x2 = x.reshape(m, d)
    z = _norm_matmul(x2, jnp.pad(w_in, ((0, 0), (0, n_pad - n_in))), g=g_mix)
    c_hg, c_gm, c_att = 0, 4 * hgw, 4 * hgw + 3 * gmw
    c_idx = c_att + 4 * attw
    o_hg, s_fin = _hgrn2(z, c_hg, lb, s0, bn, length, hgw)
    gm_out, vn = _gmlp(z, c_gm, g_gm, w_spatial, b_spatial, bn, length, gmw)
    sec = lambda c, wdt: z[:, c:c + wdt].reshape(bn, length, wdt)
    q = sec(c_att, attw).reshape(bn, length, ATT_HEADS, ATT_DIM)
    k = sec(c_att + attw, attw).reshape(bn, length, ATT_HEADS, ATT_DIM)
    v = sec(c_att + 2 * attw, attw).reshape(bn, length, ATT_HEADS, ATT_DIM)
    qi = sec(c_idx, IDX_HEADS * IDX_DIM).reshape(bn, length, IDX_HEADS, IDX_DIM)
    ki = sec(c_idx + IDX_HEADS * IDX_DIM, IDX_DIM)
    wi = sec(c_idx + IDX_HEADS * IDX_DIM + IDX_DIM, IDX_HEADS)
    att = dsa_fn(q, k, v, qi, ki, wi)
    y = _merge_matmul(x2, z, o_hg.reshape(m, hgw), gm_out, att.reshape(m, attw), g_hg, w_out,
                      (c_hg + 3 * hgw, c_gm + 2 * gmw, c_att + 3 * attw))
    return y.reshape(bn, length, d), k, v, ki, s_fin, vn.reshape(bn, length, gmw)


def _cross_layer(x, g, mk, mv, w_q, w_o):
    bn, length, d = x.shape
    m = bn * length
    x2 = x.reshape(m, d)
    q = _norm_matmul(x2, w_q, g=g).reshape(bn, length, -1)
    o = _xattn(q, mk, mv)
    return _norm_matmul(o.reshape(m, -1), w_o, res=x2).reshape(bn, length, d)


def kernel(x_prompt, x_sample, cache_k, cache_v, cache_kidx, cache_mem_k, cache_mem_v, state_hgrn, page_table, mem_prompt,
           g_mix, w_in, hg_lb, g_hg, g_gm, w_spatial, b_spatial, w_out, g_xattn, g_mem, w_xq, w_xk, w_xv, w_xo, g_final):
    depth, d_model, _ = w_in.shape
    hg_dim = w_out.shape[1] // 4 // HG_HEADS
    p_lb = jax.nn.softmax(hg_lb.astype(F32), axis=0)
    lbs = jnp.cumsum(p_lb, axis=0) - p_lb[0]
    xp, xs = x_prompt, x_sample
    bp, mem_len, _ = mem_prompt.shape
    xw = w_xk.shape[2]
    mem2 = mem_prompt.reshape(bp * mem_len, d_model)
    pk, pv, pki, phg, pmk, pmv = [], [], [], [], [], []
    sk, sv, ski, shg, sgv = [], [], [], [], []
    for l in range(depth):
        params = (g_mix[l], w_in[l], g_hg[l], g_gm[l], w_spatial[l], b_spatial[l], w_out[l])
        s0 = jnp.zeros((xp.shape[0], HG_HEADS, hg_dim, hg_dim), F32)
        xp, k_p, v_p, ki_p, s_fin, _ = _mix_layer(xp, l, lbs[l], s0, params, _dsa_prompt)
        mk = _norm_matmul(mem2, w_xk[l], g=g_mem[l]).reshape(bp, mem_len, X_HEADS, xw // X_HEADS)
        mv = _norm_matmul(mem2, w_xv[l], g=g_mem[l]).reshape(bp, mem_len, X_HEADS, xw // X_HEADS)
        xp = _cross_layer(xp, g_xattn[l], mk, mv, w_xq[l], w_xo[l])
        pk.append(k_p); pv.append(v_p); pki.append(ki_p); phg.append(s_fin); pmk.append(mk); pmv.append(mv)

        dsa_s = functools.partial(_dsa_sample, cache_k=cache_k, cache_v=cache_v, cache_kidx=cache_kidx,
                                  layer=l, page_table=page_table)
        xs, k_s, v_s, ki_s, s_new, vn_s = _mix_layer(xs, l, lbs[l], state_hgrn[l], params, dsa_s)
        xs = _cross_layer(xs, g_xattn[l], cache_mem_k[l], cache_mem_v[l], w_xq[l], w_xo[l])
        sk.append(k_s); sv.append(v_s); ski.append(ki_s); shg.append(s_new); sgv.append(vn_s)
    y_prompt = _rmsnorm(xp.reshape(-1, d_model), g_final).reshape(xp.shape)
    y_sample = _rmsnorm(xs.reshape(-1, d_model), g_final).reshape(xs.shape)
    return (y_prompt, y_sample,
            jnp.stack(pk), jnp.stack(pv), jnp.stack(pki), jnp.stack(phg), jnp.stack(pmk), jnp.stack(pmv),
            jnp.stack(sk), jnp.stack(sv), jnp.stack(ski), jnp.stack(shg), jnp.stack(sgv))
```

```python
import functools
import math

import jax
import jax.numpy as jnp
from jax import lax
from jax.experimental import pallas as pl
from jax.experimental.pallas import tpu as pltpu

F32 = jnp.float32
BF16 = jnp.bfloat16

HG_HEADS = 4
HG_CHUNK = 64
GM_GROUPS = 4
GM_CHUNK = 128
ATT_HEADS = 8
ATT_DIM = 64
IDX_HEADS = 8
IDX_DIM = 64
TOPK_MAX = 256
PAGE_SIZE = 128
X_HEADS = 4
RMS_EPS = 1e-6

LANES = 128
VMEM_LIMIT_BYTES = 48 * 1024 * 1024

INT_MIN = -(2 ** 31)
NEG_BIG = -1e30


def _norm_matmul_kernel(*refs, norm, has_res):
    if has_res:
        x_ref, g_ref, w_ref, r_ref, o_ref, h_sc = refs
    else:
        x_ref, g_ref, w_ref, o_ref, h_sc = refs

    @pl.when(pl.program_id(1) == 0)
    def _():
        x = x_ref[...]
        if norm:
            ms = jnp.mean(x * x, axis=-1, keepdims=True)
            x = x * lax.rsqrt(ms + RMS_EPS) * g_ref[...]
        h_sc[...] = x.astype(BF16)

    acc = jnp.dot(h_sc[...], w_ref[...], preferred_element_type=F32)
    if has_res:
        acc = acc + r_ref[...]
    o_ref[...] = acc


def _norm_matmul(x, w, g=None, res=None, tm=1024, tn=512):
    m, k = x.shape
    n = w.shape[1]
    tm = min(tm, m)
    tn = min(tn, n)
    assert m % tm == 0 and n % tn == 0
    norm = g is not None
    g2 = (g if norm else jnp.ones((k,), F32)).reshape(1, k).astype(F32)
    in_specs = [
        pl.BlockSpec((tm, k), lambda i, j: (i, 0)),
        pl.BlockSpec((1, k), lambda i, j: (0, 0)),
        pl.BlockSpec((k, tn), lambda i, j: (0, j)),
    ]
    args = [x, g2, w.astype(BF16)]
    if res is not None:
        in_specs.append(pl.BlockSpec((tm, tn), lambda i, j: (i, j)))
        args.append(res)
    return pl.pallas_call(
        functools.partial(_norm_matmul_kernel, norm=norm, has_res=res is not None),
        grid=(m // tm, n // tn),
        in_specs=in_specs,
        out_specs=pl.BlockSpec((tm, tn), lambda i, j: (i, j)),
        out_shape=jax.ShapeDtypeStruct((m, n), F32),
        scratch_shapes=[pltpu.VMEM((tm, k), BF16)],
        compiler_params=pltpu.CompilerParams(
            dimension_semantics=("arbitrary", "arbitrary"), vmem_limit_bytes=VMEM_LIMIT_BYTES),
    )(*args)


def _dsa_prompt_kernel(qT_ref, qiT_ref, wiT_ref, k_ref, vT_ref, ki_ref, o_ref,
                       key_sc, bias_sc, m_sc, l_sc, acc_sc, *, topk):
    tq = qT_ref.shape[4] // 2
    ck = tq
    n_pairs = k_ref.shape[1]
    dh = vT_ref.shape[3]
    i = pl.program_id(1)
    n = i + 1
    row = lax.broadcasted_iota(jnp.int32, (ck, tq), 0)
    col = lax.broadcasted_iota(jnp.int32, (ck, tq), 1)

    def chunk(c):
        return pl.ds(pl.multiple_of(c * ck, ck), ck)

    def score_body(c, carry):
        ki_c = ki_ref[0, chunk(c), :]
        ss = [jnp.dot(ki_c, qiT_ref[0, 0, :, h * tq:(h + 1) * tq], preferred_element_type=F32)
              for h in range(IDX_HEADS)]
        acc = jnp.maximum(ss[0], 0.0) * wiT_ref[0, 0, 0:1, :]
        for h in range(1, IDX_HEADS):
            acc = acc + jnp.maximum(ss[h], 0.0) * wiT_ref[0, 0, h:h + 1, :]
        bits = pltpu.bitcast(acc, jnp.int32)
        key = bits ^ ((bits >> 31) & 0x7FFFFFFF)
        key = jnp.where(key == -1, 0, key)
        adm = (row + c * ck) <= (col + i * tq)
        key_sc[chunk(c), :] = jnp.where(adm, key, INT_MIN)
        return carry

    lax.fori_loop(0, n, score_body, 0)

    def bit_body(bi, carry):
        thr, n_gt = carry
        cand = thr + lax.shift_left(jnp.int32(1), 31 - bi)

        def cnt_body(c, a):
            ge = jnp.where(key_sc[chunk(c), :] >= cand, 1.0, 0.0)
            part = ge.reshape(4, ck // 4, tq).sum(axis=0)
            return a + part.reshape(ck // 32, 8, tq).sum(axis=0)

        a = lax.fori_loop(0, n, cnt_body, jnp.zeros((8, tq), F32))
        total = a.sum(axis=0, keepdims=True)
        ok = total >= topk
        return jnp.where(ok, cand, thr), jnp.where(ok, n_gt, total)

    thr, n_gt = lax.fori_loop(
        0, 32, bit_body, (jnp.full((1, tq), INT_MIN, jnp.int32), jnp.zeros((1, tq), F32)))

    need = topk - n_gt
    tri = jnp.where(row > col, 1.0, 0.0).astype(BF16)

    def bias_body(c, eq_before):
        kk = key_sc[chunk(c), :]
        eq = kk == thr
        eqf = jnp.where(eq, 1.0, 0.0)
        rank = jnp.dot(tri, eqf.astype(BF16), preferred_element_type=F32) + eq_before
        tie = jnp.where(rank < need, jnp.where(kk != INT_MIN, 0.0, NEG_BIG), NEG_BIG)
        bias_sc[chunk(c), :] = jnp.where(kk > thr, 0.0, jnp.where(eq, tie, NEG_BIG))
        return eq_before + eqf.sum(axis=0, keepdims=True)

    lax.fori_loop(0, n, bias_body, jnp.zeros((1, tq), F32))

    m_sc[...] = jnp.full(m_sc.shape, NEG_BIG, F32)
    l_sc[...] = jnp.zeros(l_sc.shape, F32)
    acc_sc[...] = jnp.zeros(acc_sc.shape, F32)

    def att_body(c, carry):
        bias = bias_sc[chunk(c), :]
        logits = [jnp.dot(k_ref[0, pr, chunk(c), :], qT_ref[0, 0, pr], preferred_element_type=F32)
                  for pr in range(n_pairs)]
        ps, alphas = [], []
        for h in range(2 * n_pairs):
            s = logits[h // 2][:, (h % 2) * tq:(h % 2 + 1) * tq] + bias
            m_old = m_sc[h:h + 1, :]
            m_new = jnp.maximum(m_old, s.max(axis=0, keepdims=True))
            alpha = jnp.exp2(m_old - m_new)
            p = jnp.exp2(s - m_new)
            l_sc[h:h + 1, :] = alpha * l_sc[h:h + 1, :] + p.sum(axis=0, keepdims=True)
            m_sc[h:h + 1, :] = m_new
            ps.append(p.astype(BF16))
            alphas.append(alpha)
        for h in range(2 * n_pairs):
            pv = jnp.dot(vT_ref[0, h, c], ps[h], preferred_element_type=F32)
            acc_sc[h] = alphas[h] * acc_sc[h] + pv
        return carry

    lax.fori_loop(0, n, att_body, 0)
    for h in range(2 * n_pairs):
        o_ref[0, 0, h * dh:(h + 1) * dh, :] = acc_sc[h] / l_sc[h:h + 1, :]


def _dsa_prompt(q, k, v, qi, ki, wi, tq=256):
    bn, length, nh, dh = q.shape
    tq = min(tq, length)
    nq = length // tq
    topk = min(TOPK_MAX, length // 4)
    scale = (ATT_DIM ** -0.5) * math.log2(math.e)
    qT = (q * scale).astype(BF16).reshape(bn, nq, tq, nh // 2, 2, dh).transpose(0, 1, 3, 4, 5, 2)
    eye2 = jnp.eye(2, dtype=BF16)
    qT_bd = jnp.einsum('bnpedt,ef->bnpedft', qT, eye2).reshape(bn, nq, nh // 2, 2 * dh, 2 * tq)
    qiT = qi.astype(BF16).reshape(bn, nq, tq, IDX_HEADS, IDX_DIM).transpose(0, 1, 4, 3, 2)
    qiT = qiT.reshape(bn, nq, IDX_DIM, IDX_HEADS * tq)
    wiT = wi.astype(F32).reshape(bn, nq, tq, IDX_HEADS).transpose(0, 1, 3, 2)
    k_pairs = k.astype(BF16).reshape(bn, length, nh // 2, 2 * dh).transpose(0, 2, 1, 3)
    vT = v.astype(BF16).reshape(bn, nq, tq, nh, dh).transpose(0, 3, 1, 4, 2)
    ki_b = ki.astype(BF16)
    oT = pl.pallas_call(
        functools.partial(_dsa_prompt_kernel, topk=float(topk)),
        grid=(bn, nq),
        in_specs=[
            pl.BlockSpec((1, 1, nh // 2, 2 * dh, 2 * tq), lambda b, i: (b, i, 0, 0, 0)),
            pl.BlockSpec((1, 1, IDX_DIM, IDX_HEADS * tq), lambda b, i: (b, i, 0, 0)),
            pl.BlockSpec((1, 1, IDX_HEADS, tq), lambda b, i: (b, i, 0, 0)),
            pl.BlockSpec((1, nh // 2, length, 2 * dh), lambda b, i: (b, 0, 0, 0)),
            pl.BlockSpec((1, nh, nq, dh, tq), lambda b, i: (b, 0, 0, 0, 0)),
            pl.BlockSpec((1, length, IDX_DIM), lambda b, i: (b, 0, 0)),
        ],
        out_specs=pl.BlockSpec((1, 1, nh * dh, tq), lambda b, i: (b, i, 0, 0)),
        out_shape=jax.ShapeDtypeStruct((bn, nq, nh * dh, tq), F32),
        scratch_shapes=[
            pltpu.VMEM((length, tq), jnp.int32), pltpu.VMEM((length, tq), F32),
            pltpu.VMEM((nh, tq), F32), pltpu.VMEM((nh, tq), F32), pltpu.VMEM((nh, dh, tq), F32)],
        compiler_params=pltpu.CompilerParams(
            dimension_semantics=("arbitrary", "arbitrary"), vmem_limit_bytes=VMEM_LIMIT_BYTES),
    )(qT_bd, qiT, wiT, k_pairs, vT, ki_b)
    return oT.transpose(0, 1, 3, 2).reshape(bn, length, nh, dh)


QPAD = 8


def _order_key(score):
    bits = pltpu.bitcast(score, jnp.int32)
    key = bits ^ ((bits >> 31) & 0x7FFFFFFF)
    return jnp.where(key == -1, 0, key)


def _sample_index_kernel(pt_ref, qi_ref, wcol_ref, kin_ref, *rest, pp, n_new, topk):
    page_refs = rest[:pp]
    bias_ref, key_sc = rest[pp], rest[pp + 1]
    g = pl.program_id(1)
    n_pages = pl.num_programs(1) * pp
    qi = qi_ref[0]
    wcol = wcol_ref[0]

    def page_keys(page):
        s = lax.dot_general(qi, page.astype(BF16), (((1,), (1,)), ((), ())), preferred_element_type=F32)
        sc = (jnp.maximum(s, 0.0) * wcol).reshape(IDX_HEADS, QPAD, PAGE_SIZE).sum(axis=0)
        return _order_key(sc)

    for j in range(pp):
        key_sc[g * pp + j] = page_keys(page_refs[j][0, 0])

    @pl.when(g == pl.num_programs(1) - 1)
    def _():
        rowq = lax.broadcasted_iota(jnp.int32, (QPAD, PAGE_SIZE), 0)
        lane = lax.broadcasted_iota(jnp.int32, (QPAD, PAGE_SIZE), 1)
        adm_new = (lane <= rowq) & (lane < n_new)
        key_sc[n_pages] = jnp.where(adm_new, page_keys(kin_ref[0]), INT_MIN)
        n_all = n_pages + 1

        def count_ge(cand):
            cand_b = jnp.broadcast_to(cand, (QPAD, PAGE_SIZE))
            a = lax.fori_loop(
                0, n_all, lambda p, a: a + jnp.where(key_sc[p] >= cand_b, 1.0, 0.0),
                jnp.zeros((QPAD, PAGE_SIZE), F32), unroll=4)
            return a.sum(axis=1, keepdims=True)

        def bit_body(bi, carry):
            thr, n_gt = carry
            cand = thr + lax.shift_left(jnp.int32(1), 31 - bi)
            total = count_ge(cand)
            ok = total >= topk
            return jnp.where(ok, cand, thr), jnp.where(ok, n_gt, total)

        thr, n_gt = lax.fori_loop(
            0, 32, bit_body, (jnp.full((QPAD, 1), INT_MIN, jnp.int32), jnp.zeros((QPAD, 1), F32)))
        n_ge = count_ge(thr)
        thr_b = jnp.broadcast_to(thr, (QPAD, PAGE_SIZE))

        def fast_body(p, carry):
            kk = key_sc[p]
            bias_ref[0, p] = jnp.where(kk >= thr_b, jnp.where(kk != INT_MIN, 0.0, NEG_BIG), NEG_BIG)
            return carry

        lax.fori_loop(0, n_all, fast_body, 0)

        @pl.when(jnp.max(n_ge) > topk)
        def _():
            need = jnp.broadcast_to(topk - n_gt, (QPAD, PAGE_SIZE))
            r = lax.broadcasted_iota(jnp.int32, (PAGE_SIZE, PAGE_SIZE), 0)
            c = lax.broadcasted_iota(jnp.int32, (PAGE_SIZE, PAGE_SIZE), 1)
            before = jnp.where(r < c, 1.0, 0.0).astype(BF16)

            def tie_body(p, eq_before):
                kk = key_sc[p]
                eq = kk == thr_b
                eqf = jnp.where(eq, 1.0, 0.0)
                rank = jnp.dot(eqf.astype(BF16), before, preferred_element_type=F32) + eq_before
                tie = jnp.where(rank < need, jnp.where(kk != INT_MIN, 0.0, NEG_BIG), NEG_BIG)
                bias_ref[0, p] = jnp.where(kk > thr_b, 0.0, jnp.where(eq, tie, NEG_BIG))
                return eq_before + eqf.sum(axis=1, keepdims=True)

            lax.fori_loop(0, n_all, tie_body, jnp.zeros((QPAD, 1), F32))


def _sample_attn_kernel(pt_ref, q_ref, bias_ref, biasn_ref, kn_ref, vn_ref, *rest, pp):
    k_refs, v_refs = rest[:pp], rest[pp:2 * pp]
    o_ref, m_sc, l_sc, acc_sc = rest[2 * pp:]
    g = pl.program_id(1)
    qbd = q_ref[0]

    @pl.when(g == 0)
    def _():
        m_sc[...] = jnp.full(m_sc.shape, NEG_BIG, F32)
        l_sc[...] = jnp.zeros(l_sc.shape, F32)
        acc_sc[...] = jnp.zeros(acc_sc.shape, F32)

    def update(ks, vs, biases):
        width = PAGE_SIZE * len(ks)
        logits = jnp.concatenate(
            [lax.dot_general(qbd, kp.astype(BF16), (((1,), (1,)), ((), ())), preferred_element_type=F32) for kp in ks],
            axis=1)
        bias = jnp.concatenate(biases, axis=1)
        s = (logits.reshape(ATT_HEADS, QPAD, width) + bias[None]).reshape(ATT_HEADS * QPAD, width)
        m_old = m_sc[...]
        m_new = jnp.maximum(m_old, s.max(axis=1, keepdims=True))
        alpha = jnp.exp2(m_old - m_new)
        p = jnp.exp2(s - m_new)
        l_sc[...] = alpha * l_sc[...] + p.sum(axis=1, keepdims=True)
        m_sc[...] = m_new
        pv = jnp.zeros(acc_sc.shape, F32)
        for j, vp in enumerate(vs):
            pv = pv + jnp.dot(p[:, j * PAGE_SIZE:(j + 1) * PAGE_SIZE].astype(BF16), vp.astype(BF16),
                              preferred_element_type=F32)
        acc_sc[...] = alpha * acc_sc[...] + pv

    update([r[0, 0] for r in k_refs], [r[0, 0] for r in v_refs], [bias_ref[0, j] for j in range(pp)])

    @pl.when(g == pl.num_programs(1) - 1)
    def _():
        update([kn_ref[0]], [vn_ref[0]], [biasn_ref[0, 0]])
        o_ref[0] = acc_sc[...] / l_sc[...]


def _dsa_sample(q, k, v, qi, ki, wi, cache_k, cache_v, cache_kidx, layer, page_table, pp_idx=32, pp_att=16):
    bn, n_new, nh, dh = q.shape
    n_pages = page_table.shape[1]
    depth, n_pool = cache_k.shape[:2]
    topk = min(TOPK_MAX, (n_pages * PAGE_SIZE + n_new) // 4)
    pp_idx = math.gcd(pp_idx, n_pages)
    pp_att = math.gcd(pp_att, n_pages)
    pt_flat = page_table.reshape(-1).astype(jnp.int32)
    padq = lambda a: jnp.pad(a, ((0, 0), (0, 0), (0, QPAD - n_new)) + ((0, 0),) * (a.ndim - 3))
    rows_i = IDX_HEADS * QPAD
    qi_rows = padq(qi.astype(BF16).transpose(0, 2, 1, 3)).reshape(bn, rows_i, IDX_DIM)
    wcol = jnp.broadcast_to(padq(wi.astype(F32).transpose(0, 2, 1)).reshape(bn, rows_i, 1), (bn, rows_i, PAGE_SIZE))
    padk = lambda a: jnp.pad(a.astype(F32), ((0, 0), (0, PAGE_SIZE - n_new), (0, 0)))
    ki_new = padk(ki)

    def page_spec_idx(j):
        return pl.BlockSpec((1, 1, PAGE_SIZE, IDX_DIM),
                            lambda b, g, pt, j=j: (layer, pt[b * n_pages + g * pp_idx + j], 0, 0))

    bias = pl.pallas_call(
        functools.partial(_sample_index_kernel, pp=pp_idx, n_new=n_new, topk=float(topk)),
        grid_spec=pltpu.PrefetchScalarGridSpec(
            num_scalar_prefetch=1,
            grid=(bn, n_pages // pp_idx),
            in_specs=[
                pl.BlockSpec((1, rows_i, IDX_DIM), lambda b, g, pt: (b, 0, 0)),
                pl.BlockSpec((1, rows_i, PAGE_SIZE), lambda b, g, pt: (b, 0, 0)),
                pl.BlockSpec((1, PAGE_SIZE, IDX_DIM), lambda b, g, pt: (b, 0, 0)),
            ] + [page_spec_idx(j) for j in range(pp_idx)],
            out_specs=pl.BlockSpec((1, n_pages + 1, QPAD, PAGE_SIZE), lambda b, g, pt: (b, 0, 0, 0)),
            scratch_shapes=[pltpu.VMEM((n_pages + 1, QPAD, PAGE_SIZE), jnp.int32)],
        ),
        out_shape=jax.ShapeDtypeStruct((bn, n_pages + 1, QPAD, PAGE_SIZE), F32),
        compiler_params=pltpu.CompilerParams(
            dimension_semantics=("arbitrary", "arbitrary"), vmem_limit_bytes=VMEM_LIMIT_BYTES),
    )(pt_flat, qi_rows, wcol, ki_new, *([cache_kidx] * pp_idx))

    width = nh * dh
    rows_a = nh * QPAD
    scale = (ATT_DIM ** -0.5) * math.log2(math.e)
    q_rows = padq((q * scale).astype(BF16).transpose(0, 2, 1, 3))
    q_bd = jnp.einsum('bhjd,hg->bhjgd', q_rows, jnp.eye(nh, dtype=BF16)).reshape(bn, rows_a, width)
    k_new = padk(k.reshape(bn, n_new, width))
    v_new = padk(v.reshape(bn, n_new, width))
    ck = cache_k.reshape(depth, n_pool, PAGE_SIZE, width)
    cv = cache_v.reshape(depth, n_pool, PAGE_SIZE, width)

    def page_spec_att(j):
        return pl.BlockSpec((1, 1, PAGE_SIZE, width),
                            lambda b, g, pt, j=j: (layer, pt[b * n_pages + g * pp_att + j], 0, 0))

    o_rows = pl.pallas_call(
        functools.partial(_sample_attn_kernel, pp=pp_att),
        grid_spec=pltpu.PrefetchScalarGridSpec(
            num_scalar_prefetch=1,
            grid=(bn, n_pages // pp_att),
            in_specs=[
                pl.BlockSpec((1, rows_a, width), lambda b, g, pt: (b, 0, 0)),
                pl.BlockSpec((1, pp_att, QPAD, PAGE_SIZE), lambda b, g, pt: (b, g, 0, 0)),
                pl.BlockSpec((1, 1, QPAD, PAGE_SIZE), lambda b, g, pt: (b, n_pages, 0, 0)),
                pl.BlockSpec((1, PAGE_SIZE, width), lambda b, g, pt: (b, 0, 0)),
                pl.BlockSpec((1, PAGE_SIZE, width), lambda b, g, pt: (b, 0, 0)),
            ] + [page_spec_att(j) for j in range(pp_att)] * 2,
            out_specs=pl.BlockSpec((1, rows_a, width), lambda b, g, pt: (b, 0, 0)),
            scratch_shapes=[pltpu.VMEM((rows_a, 1), F32), pltpu.VMEM((rows_a, 1), F32),
                            pltpu.VMEM((rows_a, width), F32)],
        ),
        out_shape=jax.ShapeDtypeStruct((bn, rows_a, width), F32),
        compiler_params=pltpu.CompilerParams(
            dimension_semantics=("arbitrary", "arbitrary"), vmem_limit_bytes=VMEM_LIMIT_BYTES),
    )(pt_flat, q_bd, bias, bias, k_new, v_new, *([ck] * pp_att), *([cv] * pp_att))
    o = o_rows.reshape(bn, nh, QPAD, nh, dh)[:, :, :n_new]
    return jnp.einsum('bhjgd,hg->bjhd', o, jnp.eye(nh, dtype=F32))


HG_SUB = 16
HG_PAD_LOGIT = 30.0


def _hgrn_kernel(aq_ref, af_ref, ai_ref, lb_ref, s0_ref, tri_ref, o_ref, sfin_ref, st_sc, *, n_sub):
    c = pl.program_id(1)
    n_pairs = st_sc.shape[0]
    w = st_sc.shape[1]
    dk = w // 2

    @pl.when(c == 0)
    def _():
        st_sc[...] = jnp.zeros(st_sc.shape, F32)
        for p in range(n_pairs):
            for e in range(2):
                st_sc[p, e * dk:(e + 1) * dk, e * dk:(e + 1) * dk] = s0_ref[0, 2 * p + e]

    lb = lb_ref[...]
    f = lb + (1.0 - lb) * jax.nn.sigmoid(af_ref[0])
    g = jnp.log(f)
    kk = 1.0 - f
    g_hi = g.astype(BF16)
    g_lo = (g - g_hi.astype(F32)).astype(BF16)
    tri = tri_ref[...]
    b = jnp.dot(tri, g_hi, preferred_element_type=F32) + jnp.dot(tri, g_lo, preferred_element_type=F32)
    q = aq_ref[0]
    iv = ai_ref[0]

    r = lax.broadcasted_iota(jnp.int32, (w, w), 0)
    cc = lax.broadcasted_iota(jnp.int32, (w, w), 1)
    same_head = (r // dk) == (cc // dk)
    ones_bd = jnp.where(same_head, 1.0, 0.0).astype(BF16)
    mask_bd = jnp.where(same_head, 1.0, 0.0)
    trow = lax.broadcasted_iota(jnp.int32, (HG_SUB, w), 0)

    for p in range(n_pairs):
        lanes = slice(p * w, (p + 1) * w)
        q_dec, intra, upd, dec = [], [], [], []
        for j in range(n_sub):
            rows = slice(j * HG_SUB, (j + 1) * HG_SUB)
            b_blk = b[rows, lanes]
            brel = b_blk if j == 0 else b_blk - b[j * HG_SUB - 1:j * HG_SUB, lanes]
            q_blk, kk_blk, i_blk = q[rows, lanes], kk[rows, lanes], iv[rows, lanes]
            q_dec.append((q_blk * jnp.exp(brel)).astype(BF16))
            xs = []
            for s in range(HG_SUB):
                e = jnp.where(trow >= s, brel - brel[s:s + 1, :], NEG_BIG)
                xs.append((q_blk * jnp.exp(e) * kk_blk[s:s + 1, :]).astype(BF16))
            a_all = jnp.dot(jnp.concatenate(xs, axis=0), ones_bd, preferred_element_type=F32)
            o_in = a_all[0:HG_SUB] * i_blk[0:1, :]
            for s in range(1, HG_SUB):
                o_in = o_in + a_all[s * HG_SUB:(s + 1) * HG_SUB] * i_blk[s:s + 1, :]
            intra.append(o_in)
            b_last = brel[HG_SUB - 1:HG_SUB, :]
            kt = kk_blk * jnp.exp(b_last - brel)
            upd.append(mask_bd * lax.dot_general(i_blk.astype(BF16), kt.astype(BF16), (((0,), (0,)), ((), ())),
                                                 preferred_element_type=F32))
            dec.append(jnp.exp(b_last))
        st = st_sc[p]
        for j in range(n_sub):
            rows = slice(j * HG_SUB, (j + 1) * HG_SUB)
            o_ref[0, rows, lanes] = intra[j] + lax.dot_general(
                q_dec[j], st.astype(BF16), (((1,), (1,)), ((), ())), preferred_element_type=F32)
            st = dec[j] * st + upd[j]
        st_sc[p] = st

    @pl.when(c == pl.num_programs(1) - 1)
    def _():
        for p in range(n_pairs):
            for e in range(2):
                sfin_ref[0, 2 * p + e] = st_sc[p, e * dk:(e + 1) * dk, e * dk:(e + 1) * dk]


def _hgrn2(z, col0, lb, s0, bn, length, width, t_blk=128):
    dk = width // HG_HEADS
    lpad = -(-length // HG_SUB) * HG_SUB
    t_blk = min(t_blk, lpad)
    assert lpad % t_blk == 0 and col0 % width == 0
    z3 = z.reshape(bn, length, z.shape[1])
    cb = col0 // width
    if lpad != length:
        z3 = jnp.pad(z3[..., col0:col0 + 3 * width], ((0, 0), (0, lpad - length), (0, 0)),
                     constant_values=HG_PAD_LOGIT)
        cb = 0
    tri = jnp.tril(jnp.ones((t_blk, t_blk), F32)).astype(BF16)
    s0t = jnp.swapaxes(s0.astype(F32), -1, -2)
    col = lambda k: pl.BlockSpec((1, t_blk, width), lambda b, c, k=k: (b, c, cb + k))
    o, sfin_t = pl.pallas_call(
        functools.partial(_hgrn_kernel, n_sub=t_blk // HG_SUB),
        grid=(bn, lpad // t_blk),
        in_specs=[
            col(0), col(1), col(2),
            pl.BlockSpec((1, width), lambda b, c: (0, 0)),
            pl.BlockSpec((1, HG_HEADS, dk, dk), lambda b, c: (b, 0, 0, 0)),
            pl.BlockSpec((t_blk, t_blk), lambda b, c: (0, 0)),
        ],
        out_specs=[
            pl.BlockSpec((1, t_blk, width), lambda b, c: (b, c, 0)),
            pl.BlockSpec((1, HG_HEADS, dk, dk), lambda b, c: (b, 0, 0, 0)),
        ],
        out_shape=[jax.ShapeDtypeStruct((bn, lpad, width), F32),
                   jax.ShapeDtypeStruct((bn, HG_HEADS, dk, dk), F32)],
        scratch_shapes=[pltpu.VMEM((HG_HEADS // 2, 2 * dk, 2 * dk), F32)],
        compiler_params=pltpu.CompilerParams(
            dimension_semantics=("arbitrary", "arbitrary"), vmem_limit_bytes=VMEM_LIMIT_BYTES),
    )(z3, z3, z3, lb.reshape(1, width).astype(F32), s0t, tri)
    return o[:, :length], jnp.swapaxes(sfin_t, -1, -2)


def _gmlp_kernel(bu_ref, bv_ref, g_ref, w_ref, bias_ref, out_ref, vn_ref):
    u = jax.nn.gelu(bu_ref[...])
    v = jax.nn.gelu(bv_ref[...])
    vn = v * lax.rsqrt(jnp.mean(v * v, axis=-1, keepdims=True) + RMS_EPS) * g_ref[...]
    vn_ref[...] = vn
    vb = vn.astype(BF16)
    width = vn.shape[1]
    dg = width // GM_GROUPS
    group = lax.broadcasted_iota(jnp.int32, vn.shape, 1) // dg
    mixed = bias_ref[...]
    for gi in range(GM_GROUPS):
        mixed = mixed + jnp.where(group == gi, jnp.dot(w_ref[gi], vb, preferred_element_type=F32), 0.0)
    out_ref[...] = u * mixed


def _gmlp(z, col_u, g_v, w_s, b_s, bn, length, width):
    m = bn * length
    w_causal = w_s * jnp.tril(jnp.ones((GM_CHUNK, GM_CHUNK), w_s.dtype))
    dg = width // GM_GROUPS
    if length % GM_CHUNK == 0:
        w_blk = w_causal
        bias = jnp.repeat(b_s.T, dg, axis=1)
    else:
        assert GM_CHUNK % length == 0 and m % GM_CHUNK == 0
        reps = GM_CHUNK // length
        w_blk = jnp.einsum('ab,gts->gatbs', jnp.eye(reps, dtype=w_s.dtype), w_causal[:, :length, :length])
        w_blk = w_blk.reshape(GM_GROUPS, GM_CHUNK, GM_CHUNK)
        bias = jnp.tile(jnp.repeat(b_s.T[:length], dg, axis=1), (reps, 1))
    assert col_u % width == 0
    cb = col_u // width
    return pl.pallas_call(
        _gmlp_kernel,
        grid=(m // GM_CHUNK,),
        in_specs=[
            pl.BlockSpec((GM_CHUNK, width), lambda i: (i, cb)),
            pl.BlockSpec((GM_CHUNK, width), lambda i: (i, cb + 1)),
            pl.BlockSpec((1, width), lambda i: (0, 0)),
            pl.BlockSpec((GM_GROUPS, GM_CHUNK, GM_CHUNK), lambda i: (0, 0, 0)),
            pl.BlockSpec((GM_CHUNK, width), lambda i: (0, 0)),
        ],
        out_specs=[pl.BlockSpec((GM_CHUNK, width), lambda i: (i, 0))] * 2,
        out_shape=[jax.ShapeDtypeStruct((m, width), F32)] * 2,
        compiler_params=pltpu.CompilerParams(
            dimension_semantics=("arbitrary",), vmem_limit_bytes=VMEM_LIMIT_BYTES),
    )(z, z, g_v.reshape(1, width).astype(F32), w_blk.astype(BF16), bias.astype(F32))


def _xattn_kernel(q_ref, mk_ref, mv_ref, o_ref, *, n_heads):
    dh = q_ref.shape[-1] // n_heads
    scale = dh ** -0.5
    for h in range(n_heads):
        cols = slice(h * dh, (h + 1) * dh)
        qh = (q_ref[0, :, cols] * scale).astype(BF16)
        s = lax.dot_general(qh, mk_ref[0, :, cols].astype(BF16), (((1,), (1,)), ((), ())),
                            preferred_element_type=F32)
        p = jnp.exp(s - s.max(axis=-1, keepdims=True))
        o = jnp.dot(p.astype(BF16), mv_ref[0, :, cols].astype(BF16), preferred_element_type=F32)
        o_ref[0, :, cols] = o / p.sum(axis=-1, keepdims=True)


def _xattn(q, mk, mv, tm=512):
    bn, length, width = q.shape
    mem = mk.shape[1]
    tm = min(tm, length)
    assert length % tm == 0
    return pl.pallas_call(
        functools.partial(_xattn_kernel, n_heads=X_HEADS),
        grid=(bn, length // tm),
        in_specs=[
            pl.BlockSpec((1, tm, width), lambda b, i: (b, i, 0)),
            pl.BlockSpec((1, mem, width), lambda b, i: (b, 0, 0)),
            pl.BlockSpec((1, mem, width), lambda b, i: (b, 0, 0)),
        ],
        out_specs=pl.BlockSpec((1, tm, width), lambda b, i: (b, i, 0)),
        out_shape=jax.ShapeDtypeStruct((bn, length, width), F32),
        compiler_params=pltpu.CompilerParams(
            dimension_semantics=("arbitrary", "arbitrary"), vmem_limit_bytes=VMEM_LIMIT_BYTES),
    )(q, mk.reshape(bn, mem, width), mv.reshape(bn, mem, width))


def _merge_matmul_kernel(ohg_ref, gm_ref, att_ref, g1_ref, g2_ref, g3a_ref, g3b_ref, ghg_ref, w_ref, r_ref,
                         o_ref, h_sc, *, hg_dim):
    @pl.when(pl.program_id(1) == 0)
    def _():
        o = ohg_ref[...]
        wh = o.shape[1]
        r = lax.broadcasted_iota(jnp.int32, (wh, wh), 0) // hg_dim
        c = lax.broadcasted_iota(jnp.int32, (wh, wh), 1) // hg_dim
        ones_bd = jnp.where(r == c, 1.0, 0.0).astype(BF16)
        sq = o * o
        hi = sq.astype(BF16)
        lo = (sq - hi.astype(F32)).astype(BF16)
        ms = (jnp.dot(hi, ones_bd, preferred_element_type=F32)
              + jnp.dot(lo, ones_bd, preferred_element_type=F32)) * (1.0 / hg_dim)
        y = o * lax.rsqrt(ms + RMS_EPS) * ghg_ref[...]
        wg = gm_ref.shape[1]
        wa = g3a_ref.shape[1]
        h_sc[:, 0:wh] = (y * jax.nn.silu(g1_ref[...])).astype(BF16)
        h_sc[:, wh:wh + wg] = (gm_ref[...] * jax.nn.silu(g2_ref[...])).astype(BF16)
        h_sc[:, wh + wg:wh + wg + wa] = (att_ref[:, 0:wa] * jax.nn.silu(g3a_ref[...])).astype(BF16)
        h_sc[:, wh + wg + wa:] = (att_ref[:, wa:] * jax.nn.silu(g3b_ref[...])).astype(BF16)

    o_ref[...] = jnp.dot(h_sc[...], w_ref[...], preferred_element_type=F32) + r_ref[...]


def _merge_matmul(x, z, o_hg, gm_out, att, g_hg, w_out, cols, tm=1024, tn=512):
    m, d = x.shape
    wh, wg, wa = o_hg.shape[1], gm_out.shape[1], att.shape[1]
    assert wh == wg and wa == 2 * wh and all(c % wh == 0 for c in cols)
    k = wh + wg + wa
    tm = min(tm, m)
    tn = min(tn, d)
    c1, c2, c3 = (c // wh for c in cols)
    row = lambda width: pl.BlockSpec((tm, width), lambda i, j: (i, 0))
    zcol = lambda cidx: pl.BlockSpec((tm, wh), lambda i, j, cidx=cidx: (i, cidx))
    return pl.pallas_call(
        functools.partial(_merge_matmul_kernel, hg_dim=wh // HG_HEADS),
        grid=(m // tm, d // tn),
        in_specs=[row(wh), row(wg), row(wa), zcol(c1), zcol(c2), zcol(c3), zcol(c3 + 1),
                  pl.BlockSpec((1, wh), lambda i, j: (0, 0)),
                  pl.BlockSpec((k, tn), lambda i, j: (0, j)),
                  pl.BlockSpec((tm, tn), lambda i, j: (i, j))],
        out_specs=pl.BlockSpec((tm, tn), lambda i, j: (i, j)),
        out_shape=jax.ShapeDtypeStruct((m, d), F32),
        scratch_shapes=[pltpu.VMEM((tm, k), BF16)],
        compiler_params=pltpu.CompilerParams(
            dimension_semantics=("arbitrary", "arbitrary"), vmem_limit_bytes=VMEM_LIMIT_BYTES),
    )(o_hg, gm_out, att, z, z, z, z, g_hg.reshape(1, wh).astype(F32), w_out.astype(BF16), x)


def _rmsnorm_kernel(x_ref, g_ref, o_ref):
    x = x_ref[...]
    o_ref[...] = x * lax.rsqrt(jnp.mean(x * x, axis=-1, keepdims=True) + RMS_EPS) * g_ref[...]


def _rmsnorm(x, g, tm=512):
    m, d = x.shape
    tm = min(tm, m)
    return pl.pallas_call(
        _rmsnorm_kernel,
        grid=(m // tm,),
        in_specs=[pl.BlockSpec((tm, d), lambda i: (i, 0)), pl.BlockSpec((1, d), lambda i: (0, 0))],
        out_specs=pl.BlockSpec((tm, d), lambda i: (i, 0)),
        out_shape=jax.ShapeDtypeStruct((m, d), F32),
        compiler_params=pltpu.CompilerParams(dimension_semantics=("arbitrary",), vmem_limit_bytes=VMEM_LIMIT_BYTES),
    )(x, g.reshape(1, d).astype(F32))


def _mix_layer(x, layer, lb, s0, params, dsa_fn):
    g_mix, w_in, g_hg, g_gm, w_spatial, b_spatial, w_out = params
    bn, length, d = x.shape
    m = bn * length
    mix = w_out.shape[0]
    hgw, gmw, attw = mix // 4, mix // 4, mix // 2
    n_in = w_in.shape[1]
    n_pad = -(-n_in // 512) * 512
    x2 = x.reshape(m, d)
    z = _norm_matmul(x2, jnp.pad(w_in, ((0, 0), (0, n_pad - n_in))), g=g_mix)
    c_hg, c_gm, c_att = 0, 4 * hgw, 4 * hgw + 3 * gmw
    c_idx = c_att + 4 * attw
    o_hg, s_fin = _hgrn2(z, c_hg, lb, s0, bn, length, hgw)
    gm_out, vn = _gmlp(z, c_gm, g_gm, w_spatial, b_spatial, bn, length, gmw)
    sec = lambda c, wdt: z[:, c:c + wdt].reshape(bn, length, wdt)
    q = sec(c_att, attw).reshape(bn, length, ATT_HEADS, ATT_DIM)
    k = sec(c_att + attw, attw).reshape(bn, length, ATT_HEADS, ATT_DIM)
    v = sec(c_att + 2 * attw, attw).reshape(bn, length, ATT_HEADS, ATT_DIM)
    qi = sec(c_idx, IDX_HEADS * IDX_DIM).reshape(bn, length, IDX_HEADS, IDX_DIM)
    ki = sec(c_idx + IDX_HEADS * IDX_DIM, IDX_DIM)
    wi = sec(c_idx + IDX_HEADS * IDX_DIM + IDX_DIM, IDX_HEADS)
    att = dsa_fn(q, k, v, qi, ki, wi)
    y = _merge_matmul(x2, z, o_hg.reshape(m, hgw), gm_out, att.reshape(m, attw), g_hg, w_out,
                      (c_hg + 3 * hgw, c_gm + 2 * gmw, c_att + 3 * attw))
    return y.reshape(bn, length, d), k, v, ki, s_fin, vn.reshape(bn, length, gmw)


def _cross_layer(x, g, mk, mv, w_q, w_o):
    bn, length, d = x.shape
    m = bn * length
    x2 = x.reshape(m, d)
    q = _norm_matmul(x2, w_q, g=g).reshape(bn, length, -1)
    o = _xattn(q, mk, mv)
    return _norm_matmul(o.reshape(m, -1), w_o, res=x2).reshape(bn, length, d)


def kernel(x_prompt, x_sample, cache_k, cache_v, cache_kidx, cache_mem_k, cache_mem_v, state_hgrn, page_table, mem_prompt,
           g_mix, w_in, hg_lb, g_hg, g_gm, w_spatial, b_spatial, w_out, g_xattn, g_mem, w_xq, w_xk, w_xv, w_xo, g_final):
    depth, d_model, _ = w_in.shape
    hg_dim = w_out.shape[1] // 4 // HG_HEADS
    p_lb = jax.nn.softmax(hg_lb.astype(F32), axis=0)
    lbs = jnp.cumsum(p_lb, axis=0) - p_lb[0]
    xp, xs = x_prompt, x_sample
    bp, mem_len, _ = mem_prompt.shape
    xw = w_xk.shape[2]
    mem2 = mem_prompt.reshape(bp * mem_len, d_model)
    pk, pv, pki, phg, pmk, pmv = [], [], [], [], [], []
    sk, sv, ski, shg, sgv = [], [], [], [], []
    for l in range(depth):
        params = (g_mix[l], w_in[l], g_hg[l], g_gm[l], w_spatial[l], b_spatial[l], w_out[l])
        s0 = jnp.zeros((xp.shape[0], HG_HEADS, hg_dim, hg_dim), F32)
        xp, k_p, v_p, ki_p, s_fin, _ = _mix_layer(xp, l, lbs[l], s0, params, _dsa_prompt)
        mk = _norm_matmul(mem2, w_xk[l], g=g_mem[l]).reshape(bp, mem_len, X_HEADS, xw // X_HEADS)
        mv = _norm_matmul(mem2, w_xv[l], g=g_mem[l]).reshape(bp, mem_len, X_HEADS, xw // X_HEADS)
        xp = _cross_layer(xp, g_xattn[l], mk, mv, w_xq[l], w_xo[l])
        pk.append(k_p); pv.append(v_p); pki.append(ki_p); phg.append(s_fin); pmk.append(mk); pmv.append(mv)

        dsa_s = functools.partial(_dsa_sample, cache_k=cache_k, cache_v=cache_v, cache_kidx=cache_kidx,
                                  layer=l, page_table=page_table)
        xs, k_s, v_s, ki_s, s_new, vn_s = _mix_layer(xs, l, lbs[l], state_hgrn[l], params, dsa_s)
        xs = _cross_layer(xs, g_xattn[l], cache_mem_k[l], cache_mem_v[l], w_xq[l], w_xo[l])
        sk.append(k_s); sv.append(v_s); ski.append(ki_s); shg.append(s_new); sgv.append(vn_s)
    y_prompt = _rmsnorm(xp.reshape(-1, d_model), g_final).reshape(xp.shape)
    y_sample = _rmsnorm(xs.reshape(-1, d_model), g_final).reshape(xs.shape)
    return (y_prompt, y_sample,
            jnp.stack(pk), jnp.stack(pv), jnp.stack(pki), jnp.stack(phg), jnp.stack(pmk), jnp.stack(pmv),
            jnp.stack(sk), jnp.stack(sv), jnp.stack(ski), jnp.stack(shg), jnp.stack(sgv))
```
